```python
import jax, jax.numpy as jnp
from jax import lax
import numpy as np

D_MODEL = 1024
BATCH = 4
SEQ = 4096
DEPTH = 4

GRID_W = 64
CTX_LEN = 256
HEAD_DIM = 64
N_HEADS = D_MODEL // HEAD_DIM
A_HEADS = 3 * N_HEADS // 8
A_KV = 2
B_HEADS = N_HEADS // 4
C_HEADS = N_HEADS - A_HEADS - B_HEADS
C_KV = 2
MIX_WIDTH = N_HEADS * HEAD_DIM
WINDOW = 128
A_BLOCK = 128
NA_KH = 8
NA_KW = 16
NA_QCOLS = 16
NA_SLAB = NA_QCOLS + NA_KW
C_BLOCK = 128
FFN_HIDDEN = ((8 * D_MODEL + 3 * 256 - 1) // (3 * 256)) * 256
ROPE_THETA = 10000.0
EPS = 1e-6
NEG_INF = -1e30
_IN_WIDTHS = (A_HEADS * HEAD_DIM, A_KV * HEAD_DIM, A_KV * HEAD_DIM,
              B_HEADS * HEAD_DIM, B_HEADS * HEAD_DIM, B_HEADS * HEAD_DIM,
              C_HEADS * HEAD_DIM, C_KV * HEAD_DIM, C_KV * HEAD_DIM)
IN_WIDTH = sum(_IN_WIDTHS)
IN_SPLIT_POINTS = tuple(int(v) for v in np.cumsum(_IN_WIDTHS)[:-1])

kernel_name = 'hymba_style_diffusion_hybrid_block'


def rms_norm(x, g):
    xf = x.astype(jnp.float32)
    y = xf * lax.rsqrt(jnp.mean(xf * xf, axis=-1, keepdims=True) + EPS)
    return (y * g.astype(jnp.float32)).astype(x.dtype)


def modulate(h, shift, scale):
    return h * (1 + scale) + shift


def axial_rope_tables(L):
    t = jnp.arange(L, dtype=jnp.int32)
    row = (t // GRID_W).astype(jnp.float32)
    col = (t % GRID_W).astype(jnp.float32)
    n_freq = HEAD_DIM // 4
    inv = ROPE_THETA ** (-jnp.arange(n_freq, dtype=jnp.float32) / n_freq)
    ang = jnp.concatenate([row[:, None] * inv[None, :], col[:, None] * inv[None, :]], axis=-1)
    return jnp.cos(ang), jnp.sin(ang)


def apply_rope(x, cos, sin):
    half = HEAD_DIM // 2
    shape = (cos.shape[0],) + (1,) * (x.ndim - 3) + (half,)
    cos = cos.reshape(shape).astype(x.dtype)
    sin = sin.reshape(shape).astype(x.dtype)
    x1, x2 = x[..., :half], x[..., half:]
    return jnp.concatenate([x1 * cos - x2 * sin, x2 * cos + x1 * sin], axis=-1)


def split_mixer_inputs(proj, qk_g):
    B, L, _ = proj.shape
    hd = HEAD_DIM
    q_w, k_w, v_w, q_n, k_n, v_n, q_g, k_g, v_g = jnp.split(proj, IN_SPLIT_POINTS, axis=-1)
    q_w = rms_norm(q_w.reshape(B, L, A_KV, A_HEADS // A_KV, hd), qk_g[0])
    k_w = rms_norm(k_w.reshape(B, L, A_KV, hd), qk_g[1])
    v_w = v_w.reshape(B, L, A_KV, hd)
    q_n = rms_norm(q_n.reshape(B, L, B_HEADS, 1, hd), qk_g[2])
    k_n = rms_norm(k_n.reshape(B, L, B_HEADS, hd), qk_g[3])
    v_n = v_n.reshape(B, L, B_HEADS, hd)
    q_g = rms_norm(q_g.reshape(B, L, C_KV, C_HEADS // C_KV, hd), qk_g[4])
    k_g = rms_norm(k_g.reshape(B, L, C_KV, hd), qk_g[5])
    v_g = v_g.reshape(B, L, C_KV, hd)
    return q_w, k_w, v_w, q_n, k_n, v_n, q_g, k_g, v_g


def context_attention(q, k, v, sink=None):
    B, Lq, Hkv, G, dh = q.shape
    s = jnp.einsum('bqhgd,bkhd->bhgqk', q, k).astype(jnp.float32) * (dh ** -0.5)
    if sink is not None:
        s_sink = jnp.broadcast_to(sink.astype(jnp.float32).reshape(1, Hkv, G, 1, 1), s.shape[:-1] + (1,))
        s = jnp.concatenate([s, s_sink], axis=-1)
    p = jax.nn.softmax(s, axis=-1)[..., :k.shape[1]].astype(v.dtype)
    o = jnp.einsum('bhgqk,bkhd->bqhgd', p, v)
    return o.reshape(B, Lq, Hkv * G * dh)


def window_attention(q, k, v, kt, vt, sink):
    B, L, Hkv, G, dh = q.shape
    nb = L // A_BLOCK
    qb = q.reshape(B, nb, A_BLOCK, Hkv, G, dh)
    pad = ((0, 0), (A_BLOCK, A_BLOCK), (0, 0), (0, 0))
    kp = jnp.pad(k, pad).reshape(B, nb + 2, A_BLOCK, Hkv, dh)
    vp = jnp.pad(v, pad).reshape(B, nb + 2, A_BLOCK, Hkv, dh)
    kb = jnp.concatenate([kp[:, :-2], kp[:, 1:-1], kp[:, 2:]], axis=2)
    vb = jnp.concatenate([vp[:, :-2], vp[:, 1:-1], vp[:, 2:]], axis=2)
    qpos = jnp.arange(L).reshape(nb, A_BLOCK)
    kpos = jnp.arange(nb)[:, None] * A_BLOCK + jnp.arange(-A_BLOCK, 2 * A_BLOCK)[None, :]
    valid = ((kpos[:, None, :] >= 0) & (kpos[:, None, :] < L)
             & (jnp.abs(qpos[:, :, None] - kpos[:, None, :]) <= WINDOW))
    scale = dh ** -0.5
    s_loc = jnp.einsum('bnqhgd,bnkhd->bnhgqk', qb, kb).astype(jnp.float32) * scale
    s_loc = jnp.where(valid[None, :, None, None], s_loc, NEG_INF)
    s_ctx = jnp.einsum('bnqhgd,bkhd->bnhgqk', qb, kt).astype(jnp.float32) * scale
    s_sink = jnp.broadcast_to(sink.astype(jnp.float32).reshape(1, 1, Hkv, G, 1, 1), s_loc.shape[:-1] + (1,))
    p = jax.nn.softmax(jnp.concatenate([s_loc, s_ctx, s_sink], axis=-1), axis=-1).astype(v.dtype)
    n_loc = 3 * A_BLOCK
    n_ctx = kt.shape[1]
    o = (jnp.einsum('bnhgqk,bnkhd->bnqhgd', p[..., :n_loc], vb)
         + jnp.einsum('bnhgqk,bkhd->bnqhgd', p[..., n_loc:n_loc + n_ctx], vt))
    return o.reshape(B, L, Hkv * G * dh)


def neighbourhood_attention(q, k, v, kt, vt, rpb):
    B, L, H, _, dh = q.shape
    rows = L // GRID_W
    kh = min(NA_KH, rows)
    ncb = GRID_W // NA_QCOLS
    r = jnp.arange(rows)
    key_rows = jnp.clip(r - kh // 2, 0, rows - kh)[:, None] + jnp.arange(kh)[None, :]
    q_cols = jnp.arange(ncb)[:, None] * NA_QCOLS + jnp.arange(NA_QCOLS)[None, :]
    key_cols = (jnp.clip(jnp.arange(ncb) * NA_QCOLS - NA_KW // 2, 0, GRID_W - NA_SLAB)[:, None]
                + jnp.arange(NA_SLAB)[None, :])
    win_start = jnp.clip(q_cols - NA_KW // 2, 0, GRID_W - NA_KW)
    col_in = ((key_cols[:, None, :] >= win_start[:, :, None])
              & (key_cols[:, None, :] < win_start[:, :, None] + NA_KW))
    n_loc = kh * NA_SLAB
    valid = jnp.broadcast_to(col_in[:, :, None, :], (ncb, NA_QCOLS, kh, NA_SLAB)).reshape(ncb, NA_QCOLS, n_loc)
    idx_r = key_rows[:, None, :, None]
    idx_c = key_cols[None, :, None, :]
    k_nb = k.reshape(B, rows, GRID_W, H, dh)[:, idx_r, idx_c].reshape(B, rows, ncb, n_loc, H, dh)
    v_nb = v.reshape(B, rows, GRID_W, H, dh)[:, idx_r, idx_c].reshape(B, rows, ncb, n_loc, H, dh)
    qg = q.reshape(B, rows, ncb, NA_QCOLS, H, dh)
    dr = key_rows - r[:, None] + (NA_KH - 1)
    dc = jnp.clip(key_cols[:, None, :] - q_cols[:, :, None] + (NA_KW - 1), 0, 2 * NA_KW - 2)
    bias = rpb[:, dr[:, None, None, :, None], dc[None, :, :, None, :]]
    bias = jnp.transpose(bias.reshape(H, rows, ncb, NA_QCOLS, n_loc), (1, 2, 0, 3, 4)).astype(jnp.float32)
    scale = dh ** -0.5
    s_loc = jnp.einsum('brcqhd,brckhd->brchqk', qg, k_nb).astype(jnp.float32) * scale + bias[None]
    s_loc = jnp.where(valid[None, None, :, None], s_loc, NEG_INF)
    s_ctx = jnp.einsum('brcqhd,bkhd->brchqk', qg, kt).astype(jnp.float32) * scale
    p = jax.nn.softmax(jnp.concatenate([s_loc, s_ctx], axis=-1), axis=-1).astype(v.dtype)
    o = (jnp.einsum('brchqk,brckhd->brcqhd', p[..., :n_loc], v_nb)
         + jnp.einsum('brchqk,bkhd->brcqhd', p[..., n_loc:], vt))
    return o.reshape(B, L, H * dh)


def global_block_attention(q, k, v, kt, vt):
    B, L, Hkv, G, dh = q.shape
    nb = L // C_BLOCK
    qb = jnp.moveaxis(q.reshape(B, nb, C_BLOCK, Hkv, G, dh), 1, 0)
    k_all = jnp.concatenate([k, kt], axis=1)
    v_all = jnp.concatenate([v, vt], axis=1)
    scale = dh ** -0.5

    def block(q_blk):
        s = jnp.einsum('bqhgd,bkhd->bhgqk', q_blk, k_all).astype(jnp.float32) * scale
        p = jax.nn.softmax(s, axis=-1).astype(v_all.dtype)
        return jnp.einsum('bhgqk,bkhd->bqhgd', p, v_all)

    o = lax.map(block, qb)
    return jnp.moveaxis(o, 0, 1).reshape(B, L, Hkv * G * dh)


def swiglu(h, w_gate, w_up, w_down):
    return (jax.nn.silu(h @ w_gate) * (h @ w_up)) @ w_down


def trunk_layer(x, xt, silu_c, silu_ct, w_mod, b_mod, g_attn, g_ffn, w_in, qk_g, sink, rpb,
                w_out, w_gate, w_up, w_down, cos, sin, update_ctx):
    sh_a, sc_a, ga_a, sh_f, sc_f, ga_f = jnp.split((silu_c @ w_mod + b_mod)[:, None, :], 6, axis=-1)
    tsh_a, tsc_a, tga_a, tsh_f, tsc_f, tga_f = jnp.split(silu_ct @ w_mod + b_mod, 6, axis=-1)
    h = modulate(rms_norm(x, g_attn), sh_a, sc_a)
    ht = modulate(rms_norm(xt, g_attn), tsh_a, tsc_a)
    q_w, k_w, v_w, q_n, k_n, v_n, q_g, k_g, v_g = split_mixer_inputs(h @ w_in, qk_g)
    tq_w, tk_w, tv_w, tq_n, tk_n, tv_n, tq_g, tk_g, tv_g = split_mixer_inputs(ht @ w_in, qk_g)
    q_w, k_w = apply_rope(q_w, cos, sin), apply_rope(k_w, cos, sin)
    q_g, k_g = apply_rope(q_g, cos, sin), apply_rope(k_g, cos, sin)
    o = jnp.concatenate([window_attention(q_w, k_w, v_w, tk_w, tv_w, sink),
                         neighbourhood_attention(q_n, k_n, v_n, tk_n, tv_n, rpb),
                         global_block_attention(q_g, k_g, v_g, tk_g, tv_g)], axis=-1)
    x = x + ga_a * (o @ w_out)
    x = x + ga_f * swiglu(modulate(rms_norm(x, g_ffn), sh_f, sc_f), w_gate, w_up, w_down)
    if update_ctx:
        ot = jnp.concatenate([context_attention(tq_w, tk_w, tv_w, sink),
                              context_attention(tq_n, tk_n, tv_n),
                              context_attention(tq_g, tk_g, tv_g)], axis=-1)
        xt = xt + tga_a * (ot @ w_out)
        xt = xt + tga_f * swiglu(modulate(rms_norm(xt, g_ffn), tsh_f, tsc_f), w_gate, w_up, w_down)
    return x, xt


def setup_inputs(seed: int = 0) -> dict:
    key = jax.random.key(seed)
    ks = jax.random.split(key, 17)
    f32 = jnp.float32

    def nrm(k, shape, s):
        return jax.random.normal(k, shape, f32) * s

    return {
        'x': nrm(ks[0], (BATCH, SEQ, D_MODEL), 1.0),
        'c': nrm(ks[1], (BATCH, D_MODEL), 1.0),
        'ctx': nrm(ks[2], (BATCH, CTX_LEN, D_MODEL), 1.0),
        'c_ctx': nrm(ks[3], (D_MODEL,), 1.0),
        'w_mod': nrm(ks[4], (DEPTH, D_MODEL, 6 * D_MODEL), 0.5 * D_MODEL ** -0.5),
        'b_mod': nrm(ks[5], (DEPTH, 6 * D_MODEL), 0.01),
        'norm_attn': 1.0 + nrm(ks[6], (DEPTH, D_MODEL), 0.01),
        'norm_ffn': 1.0 + nrm(ks[7], (DEPTH, D_MODEL), 0.01),
        'w_in': nrm(ks[8], (DEPTH, D_MODEL, IN_WIDTH), D_MODEL ** -0.5),
        'qk_norm': 1.0 + nrm(ks[9], (DEPTH, 6, HEAD_DIM), 0.01),
        'sink': nrm(ks[10], (DEPTH, A_HEADS), 0.5),
        'rpb': nrm(ks[11], (DEPTH, B_HEADS, 2 * NA_KH - 1, 2 * NA_KW - 1), 0.1),
        'w_out': nrm(ks[12], (DEPTH, MIX_WIDTH, D_MODEL), MIX_WIDTH ** -0.5),
        'w_gate': nrm(ks[13], (DEPTH, D_MODEL, FFN_HIDDEN), D_MODEL ** -0.5),
        'w_up': nrm(ks[14], (DEPTH, D_MODEL, FFN_HIDDEN), D_MODEL ** -0.5),
        'w_down': nrm(ks[15], (DEPTH, FFN_HIDDEN, D_MODEL), FFN_HIDDEN ** -0.5),
    }


def reference(x, c, ctx, c_ctx, w_mod, b_mod, norm_attn, norm_ffn, w_in, qk_norm, sink, rpb,
              w_out, w_gate, w_up, w_down):
    L = x.shape[1]
    cos, sin = axial_rope_tables(L)
    silu_c = jax.nn.silu(c)
    silu_ct = jax.nn.silu(c_ctx)
    xt = ctx
    for i in range(DEPTH):
        x, xt = trunk_layer(x, xt, silu_c, silu_ct, w_mod[i], b_mod[i], norm_attn[i], norm_ffn[i],
                            w_in[i], qk_norm[i], sink[i], rpb[i], w_out[i], w_gate[i], w_up[i], w_down[i],
                            cos, sin, i < DEPTH - 1)
    return x
```

```python
import functools

import numpy as np
import jax
import jax.numpy as jnp
from jax import lax
from jax.experimental import pallas as pl
from jax.experimental.pallas import tpu as pltpu

D_MODEL = 1024
SEQ = 4096
CTX_LEN = 256
TOK = SEQ + CTX_LEN
GRID_W = 64
ROWS = SEQ // GRID_W
HEAD_DIM = 64
A_HEADS = 6
B_HEADS = 4
C_HEADS = 6
GQA_G = 3
WINDOW = 128
NA_KH = 8
NA_KW = 16
FFN_HIDDEN = 2816
IN_WIDTH = 2048
ROPE_THETA = 10000.0
EPS = 1e-6
NEG_INF = -1e30
MOD_ROWS = 8
CTX_ROW = 4

LANE = 128
TM = 256
N_LAT_TILES = SEQ // TM
TQ = 128
NQ = SEQ // TQ
Q_ROWS = TQ // GRID_W
NB_ROWS = Q_ROWS + NA_KH
NB_KEYS = NB_ROWS * GRID_W
A_SPAN = TQ + 2 * WINDOW
C_CHUNK = 512
VMEM_LIMIT = 56 * 1024 * 1024

COL_QA, COL_QC, COL_KA, COL_KC = 0, 384, 768, 896
COL_QB, COL_KB = 1024, 1280
COL_VA, COL_VC, COL_VB = 1536, 1664, 1792
OCOL_A, OCOL_B, OCOL_C = 0, 384, 640

F32 = jnp.float32
BF16 = jnp.bfloat16
NT_DIMS = (((1,), (1,)), ((), ()))


def _gqa_cols(base):
    cols = np.empty(GQA_G * 2 * HEAD_DIM, np.int32)
    for j in range(GQA_G):
        for g in range(2):
            dst = j * LANE + g * HEAD_DIM
            cols[dst:dst + HEAD_DIM] = base + (g * GQA_G + j) * HEAD_DIM + np.arange(HEAD_DIM)
    return cols


def _in_perm():
    r = np.arange
    return np.concatenate([
        _gqa_cols(0),
        _gqa_cols(1408),
        384 + r(128),
        1792 + r(128),
        640 + r(256),
        896 + r(256),
        512 + r(128),
        1920 + r(128),
        1152 + r(256),
    ]).astype(np.int32)


def _out_perm():
    return np.concatenate([_gqa_cols(0), 384 + np.arange(256), _gqa_cols(640)]).astype(np.int32)


def _gain_vector(qk):
    sc = HEAD_DIM ** -0.5
    t = jnp.tile
    return jnp.concatenate([
        t(qk[0] * sc, 6), t(qk[4] * sc, 6), t(qk[1], 2), t(qk[5], 2),
        t(qk[2] * sc, 4), t(qk[3], 4), jnp.ones((512,), F32)])


def _rope_tables():
    t = jnp.arange(SEQ, dtype=jnp.int32)
    row = (t // GRID_W).astype(F32)
    col = (t % GRID_W).astype(F32)
    n_freq = HEAD_DIM // 4
    inv = ROPE_THETA ** (-jnp.arange(n_freq, dtype=F32) / n_freq)
    ang = jnp.concatenate([row[:, None] * inv[None, :], col[:, None] * inv[None, :]], axis=-1)
    cos, sin = jnp.cos(ang), jnp.sin(ang)
    cos = jnp.concatenate([cos, jnp.ones((CTX_LEN, HEAD_DIM // 2), F32)], axis=0)
    sin = jnp.concatenate([sin, jnp.zeros((CTX_LEN, HEAD_DIM // 2), F32)], axis=0)
    cos_t = jnp.tile(cos, (1, 2 * LANE // HEAD_DIM))
    sin_t = jnp.tile(jnp.concatenate([-sin, sin], axis=-1), (1, LANE // HEAD_DIM))
    return cos_t, sin_t


def _nb_tables():
    pats = []
    for i in range(NQ):
        ks = int(np.clip(Q_ROWS * i - NA_KH // 2, 0, ROWS - NB_ROWS))
        qi = np.arange(TQ)
        r = Q_ROWS * i + qi // GRID_W
        c = qi % GRID_W
        kk = np.arange(NB_KEYS)
        krow = ks + kk // GRID_W
        kcol = kk % GRID_W
        rs = np.clip(r - NA_KH // 2, 0, ROWS - NA_KH)
        ws = np.clip(c - NA_KW // 2, 0, GRID_W - NA_KW)
        valid = ((krow[None, :] >= rs[:, None]) & (krow[None, :] < rs[:, None] + NA_KH)
                 & (kcol[None, :] >= ws[:, None]) & (kcol[None, :] < ws[:, None] + NA_KW))
        dr = np.clip(krow[None, :] - r[:, None] + (NA_KH - 1), 0, 2 * NA_KH - 2)
        dc = np.clip(kcol[None, :] - c[:, None] + (NA_KW - 1), 0, 2 * NA_KW - 2)
        pats.append(np.where(valid, dr * (2 * NA_KW - 1) + dc, -1).astype(np.int32))
    pats = np.stack(pats)
    uniq, inv = np.unique(pats.reshape(NQ, -1), axis=0, return_inverse=True)
    uniq = uniq.reshape(-1, TQ, NB_KEYS)
    return uniq, inv.reshape(-1).astype(np.int32)


def _mod_kernel(c_ref, w_ref, b_ref, o_ref):
    c = c_ref[...]
    a = (c / (1.0 + jnp.exp(-c))).astype(BF16)
    o_ref[0] = jnp.dot(a, w_ref[0].astype(BF16), preferred_element_type=F32) + b_ref[0]


def _modulation(c8, w_mod, b_mod):
    depth = w_mod.shape[0]
    tn = 2048
    return pl.pallas_call(
        _mod_kernel,
        out_shape=jax.ShapeDtypeStruct((depth, MOD_ROWS, 6 * D_MODEL), F32),
        grid=(depth, 6 * D_MODEL // tn),
        in_specs=[pl.BlockSpec((MOD_ROWS, D_MODEL), lambda l, n: (0, 0)),
                  pl.BlockSpec((1, D_MODEL, tn), lambda l, n: (l, 0, n)),
                  pl.BlockSpec((1, 1, tn), lambda l, n: (l, 0, n))],
        out_specs=pl.BlockSpec((1, MOD_ROWS, tn), lambda l, n: (l, 0, n)),
        compiler_params=pltpu.CompilerParams(
            dimension_semantics=("arbitrary", "arbitrary"), vmem_limit_bytes=VMEM_LIMIT),
        name="modulation",
    )(c8, w_mod, b_mod.reshape(depth, 1, 6 * D_MODEL))


def _mod_spec(chunk):
    return pl.BlockSpec((1, 1, D_MODEL),
                        lambda b, i: (jnp.where(i < N_LAT_TILES, b, CTX_ROW), 0, chunk))


def _resident_spec(shape):
    return pl.BlockSpec(shape, lambda b, i: (0,) * len(shape), pipeline_mode=pl.Buffered(1))


def _norm_modulate(x, g, scale, shift):
    ms = jnp.mean(x * x, axis=-1, keepdims=True)
    return x * lax.rsqrt(ms + EPS) * (g * (1.0 + scale)) + shift


def _inproj_kernel(x_ref, sh_ref, sc_ref, g_ref, w_ref, gv_ref, bd_ref, cos_ref, sin_ref, o_ref):
    h = _norm_modulate(x_ref[0], g_ref[...], sc_ref[0], sh_ref[0]).astype(BF16)
    lane = lax.broadcasted_iota(jnp.int32, (TM, LANE), 1)
    first_half = (lane % HEAD_DIM) < (HEAD_DIM // 2)
    cos = cos_ref[...]
    sin = sin_ref[...]
    cw = 2 * LANE
    for c in range(IN_WIDTH // cw):
        cs = slice(c * cw, (c + 1) * cw)
        p = jnp.dot(h, w_ref[:, cs], preferred_element_type=F32)
        if c * cw < COL_VA:
            ss = jnp.dot((p * p).astype(BF16), bd_ref[...], preferred_element_type=F32)
            p = p * lax.rsqrt(ss * (1.0 / HEAD_DIM) + EPS) * gv_ref[:, cs]
        if c * cw < COL_QB:
            halves = []
            for s in range(2):
                xh = p[:, s * LANE:(s + 1) * LANE]
                partner = jnp.where(first_half,
                                    pltpu.roll(xh, LANE - HEAD_DIM // 2, 1),
                                    pltpu.roll(xh, HEAD_DIM // 2, 1))
                halves.append(xh * cos + partner * sin)
            p = jnp.concatenate(halves, axis=1)
        o_ref[0, :, cs] = p.astype(BF16)


def _in_projection(xx, mod3, g_attn, w_in, gvec, bd, cos_t, sin_t):
    nb = xx.shape[0]
    full = _resident_spec
    return pl.pallas_call(
        _inproj_kernel,
        out_shape=jax.ShapeDtypeStruct((nb, TOK, IN_WIDTH), BF16),
        grid=(nb, TOK // TM),
        in_specs=[pl.BlockSpec((1, TM, D_MODEL), lambda b, i: (b, i, 0)),
                  _mod_spec(0), _mod_spec(1),
                  full((1, D_MODEL)), full((D_MODEL, IN_WIDTH)), full((1, IN_WIDTH)),
                  full((2 * LANE, 2 * LANE)),
                  pl.BlockSpec((TM, LANE), lambda b, i: (i, 0)),
                  pl.BlockSpec((TM, LANE), lambda b, i: (i, 0))],
        out_specs=pl.BlockSpec((1, TM, IN_WIDTH), lambda b, i: (b, i, 0)),
        compiler_params=pltpu.CompilerParams(
            dimension_semantics=("arbitrary", "arbitrary"), vmem_limit_bytes=VMEM_LIMIT),
        name="in_projection",
    )(xx, mod3, mod3, g_attn, w_in, gvec, bd, cos_t, sin_t)


def _stack_gqa(q_ref, nq):
    low = lax.broadcasted_iota(jnp.int32, (nq, LANE), 1) < HEAD_DIM
    blocks = []
    for g in range(2):
        for j in range(GQA_G):
            qj = q_ref[0, :, j * LANE:(j + 1) * LANE]
            blocks.append(jnp.where(low if g == 0 else jnp.logical_not(low), qj, jnp.zeros_like(qj)))
    return jnp.concatenate(blocks, axis=0)


def _unstack_gqa(o, nq):
    low = lax.broadcasted_iota(jnp.int32, (nq, LANE), 1) < HEAD_DIM
    return [jnp.where(low, o[j * nq:(j + 1) * nq], o[(GQA_G + j) * nq:(GQA_G + j + 1) * nq])
            for j in range(GQA_G)]


def _stack_mha(q_ref, nq):
    q = q_ref[0]
    head = lax.broadcasted_iota(jnp.int32, (nq, B_HEADS * HEAD_DIM), 1) // HEAD_DIM
    return jnp.concatenate([jnp.where(head == h, q, jnp.zeros_like(q)) for h in range(B_HEADS)], axis=0)


def _unstack_mha(o, nq):
    head = lax.broadcasted_iota(jnp.int32, (nq, B_HEADS * HEAD_DIM), 1) // HEAD_DIM
    out = jnp.where(head == 0, o[0:nq], 0.0)
    for h in range(1, B_HEADS):
        out = jnp.where(head == h, o[h * nq:(h + 1) * nq], out)
    return out


def _qk(q, k):
    return lax.dot_general(q, k, NT_DIMS, preferred_element_type=F32)


def _softmax_pv(parts, extra_logit=None):
    m = None
    for s, _ in parts:
        sm = jnp.max(s, axis=-1, keepdims=True)
        m = sm if m is None else jnp.maximum(m, sm)
    if extra_logit is not None:
        m = jnp.maximum(m, extra_logit)
    den = None if extra_logit is None else jnp.exp(extra_logit - m)
    acc = None
    for s, v in parts:
        p = jnp.exp(s - m)
        ps = jnp.sum(p, axis=-1, keepdims=True)
        den = ps if den is None else den + ps
        pv = jnp.dot(p.astype(BF16), v, preferred_element_type=F32)
        acc = pv if acc is None else acc + pv
    return acc * (1.0 / den)


def _sink_column(sink_ref, nq):
    blk = lax.broadcasted_iota(jnp.int32, (A_HEADS * nq, 1), 0) // nq
    col = jnp.full((A_HEADS * nq, 1), sink_ref[0], F32)
    for n in range(1, A_HEADS):
        col = jnp.where(blk == n, sink_ref[n], col)
    return col


def _global_attention(q, k_ref, v_ref):
    bounds = [(k0, C_CHUNK) for k0 in range(0, SEQ, C_CHUNK)] + [(SEQ, CTX_LEN)]
    m = l = acc = None
    for k0, kn in bounds:
        s = _qk(q, k_ref[0, k0:k0 + kn, :])
        sm = jnp.max(s, axis=-1, keepdims=True)
        if m is None:
            m_new = sm
        else:
            m_new = jnp.maximum(m, sm)
            alpha = jnp.exp(m - m_new)
        p = jnp.exp(s - m_new)
        ps = jnp.sum(p, axis=-1, keepdims=True)
        pv = jnp.dot(p.astype(BF16), v_ref[0, k0:k0 + kn, :], preferred_element_type=F32)
        if m is None:
            l, acc = ps, pv
        else:
            l = alpha * l + ps
            acc = alpha * acc + pv
        m = m_new
    return acc * (1.0 / l)


def _attn_kernel(tbl_ref, sink_ref, qa_ref, qc_ref, qb_ref, ka_ref, va_ref, kc_ref, vc_ref,
                 kb_ref, vb_ref, bias_ref, o_ref):
    del tbl_ref
    i = pl.program_id(1)
    ctx = slice(SEQ, TOK)

    qa = _stack_gqa(qa_ref, TQ)
    start = pl.multiple_of(jnp.clip(i * TQ - WINDOW, 0, SEQ - A_SPAN), LANE)
    qpos = i * TQ + lax.broadcasted_iota(jnp.int32, (TQ, A_SPAN), 0)
    kpos = start + lax.broadcasted_iota(jnp.int32, (TQ, A_SPAN), 1)
    valid = jnp.abs(qpos - kpos) <= WINDOW
    s_loc = _qk(qa, ka_ref[0, pl.ds(start, A_SPAN), :]).reshape(A_HEADS, TQ, A_SPAN)
    s_loc = jnp.where(valid[None], s_loc, NEG_INF).reshape(A_HEADS * TQ, A_SPAN)
    s_ctx = _qk(qa, ka_ref[0, ctx, :])
    oa = _softmax_pv([(s_loc, va_ref[0, pl.ds(start, A_SPAN), :]), (s_ctx, va_ref[0, ctx, :])],
                     _sink_column(sink_ref, TQ))
    for j, blk in enumerate(_unstack_gqa(oa, TQ)):
        o_ref[0, :, OCOL_A + j * LANE:OCOL_A + (j + 1) * LANE] = blk.astype(BF16)

    qb = _stack_mha(qb_ref, TQ)
    krow0 = jnp.clip(Q_ROWS * i - NA_KH // 2, 0, ROWS - NB_ROWS)
    kstart = pl.multiple_of(krow0 * GRID_W, GRID_W)
    s_loc = _qk(qb, kb_ref[0, pl.ds(kstart, NB_KEYS), :]) + bias_ref[0].reshape(B_HEADS * TQ, NB_KEYS)
    s_ctx = _qk(qb, kb_ref[0, ctx, :])
    ob = _softmax_pv([(s_loc, vb_ref[0, pl.ds(kstart, NB_KEYS), :]), (s_ctx, vb_ref[0, ctx, :])])
    o_ref[0, :, OCOL_B:OCOL_C] = _unstack_mha(ob, TQ).astype(BF16)

    oc = _global_attention(_stack_gqa(qc_ref, TQ), kc_ref, vc_ref)
    for j, blk in enumerate(_unstack_gqa(oc, TQ)):
        o_ref[0, :, OCOL_C + j * LANE:OCOL_C + (j + 1) * LANE] = blk.astype(BF16)


def _attention(qkv, sink, bias_tbl, tbl_idx):
    nb = qkv.shape[0]
    qspec = lambda w, cb: pl.BlockSpec((1, TQ, w), lambda b, i, t: (b, i, cb))
    kvspec = lambda w, cb: pl.BlockSpec((1, TOK, w), lambda b, i, t: (b, 0, cb))
    grid_spec = pltpu.PrefetchScalarGridSpec(
        num_scalar_prefetch=1,
        grid=(nb, NQ),
        in_specs=[pl.BlockSpec(memory_space=pltpu.SMEM),
                  qspec(384, COL_QA // 384), qspec(384, COL_QC // 384), qspec(256, COL_QB // 256),
                  kvspec(128, COL_KA // 128), kvspec(128, COL_VA // 128),
                  kvspec(128, COL_KC // 128), kvspec(128, COL_VC // 128),
                  kvspec(256, COL_KB // 256), kvspec(256, COL_VB // 256),
                  pl.BlockSpec((1, B_HEADS, TQ, NB_KEYS), lambda b, i, t: (t[i], 0, 0, 0))],
        out_specs=pl.BlockSpec((1, TQ, D_MODEL), lambda b, i, t: (b, i, 0)),
    )
    return pl.pallas_call(
        _attn_kernel,
        out_shape=jax.ShapeDtypeStruct((nb, TOK, D_MODEL), BF16),
        grid_spec=grid_spec,
        compiler_params=pltpu.CompilerParams(
            dimension_semantics=("arbitrary", "arbitrary"), vmem_limit_bytes=VMEM_LIMIT),
        name="attention",
    )(tbl_idx, sink, qkv, qkv, qkv, qkv, qkv, qkv, qkv, qkv, qkv, bias_tbl)


def _ctx_attn_kernel(sink_ref, qa_ref, qc_ref, qb_ref, ka_ref, va_ref, kc_ref, vc_ref,
                     kb_ref, vb_ref, prev_ref, o_ref):
    del prev_ref
    n = CTX_LEN
    qa = _stack_gqa(qa_ref, n)
    oa = _softmax_pv([(_qk(qa, ka_ref[0]), va_ref[0])], _sink_column(sink_ref, n))
    for j, blk in enumerate(_unstack_gqa(oa, n)):
        o_ref[0, :, OCOL_A + j * LANE:OCOL_A + (j + 1) * LANE] = blk.astype(BF16)
    qb = _stack_mha(qb_ref, n)
    ob = _softmax_pv([(_qk(qb, kb_ref[0]), vb_ref[0])])
    o_ref[0, :, OCOL_B:OCOL_C] = _unstack_mha(ob, n).astype(BF16)
    qc = _stack_gqa(qc_ref, n)
    oc = _softmax_pv([(_qk(qc, kc_ref[0]), vc_ref[0])])
    for j, blk in enumerate(_unstack_gqa(oc, n)):
        o_ref[0, :, OCOL_C + j * LANE:OCOL_C + (j + 1) * LANE] = blk.astype(BF16)


def _ctx_attention(qkv, sink, o_prev):
    nb = qkv.shape[0]
    rb = SEQ // CTX_LEN
    spec = lambda w, cb: pl.BlockSpec((1, CTX_LEN, w), lambda b: (b, rb, cb))
    return pl.pallas_call(
        _ctx_attn_kernel,
        out_shape=jax.ShapeDtypeStruct((nb, TOK, D_MODEL), BF16),
        grid=(nb,),
        in_specs=[pl.BlockSpec(memory_space=pltpu.SMEM),
                  spec(384, COL_QA // 384), spec(384, COL_QC // 384), spec(256, COL_QB // 256),
                  spec(128, COL_KA // 128), spec(128, COL_VA // 128),
                  spec(128, COL_KC // 128), spec(128, COL_VC // 128),
                  spec(256, COL_KB // 256), spec(256, COL_VB // 256),
                  pl.BlockSpec(memory_space=pl.ANY)],
        out_specs=pl.BlockSpec((1, CTX_LEN, D_MODEL), lambda b: (b, rb, 0)),
        input_output_aliases={10: 0},
        compiler_params=pltpu.CompilerParams(
            dimension_semantics=("arbitrary",), vmem_limit_bytes=VMEM_LIMIT),
        name="ctx_attention",
    )(sink, qkv, qkv, qkv, qkv, qkv, qkv, qkv, qkv, qkv, o_prev)


def _ffn_kernel(o_ref, x_ref, gaa_ref, shf_ref, scf_ref, gaf_ref, g_ref,
                wo_ref, wg_ref, wu_ref, wd_ref, out_ref):
    a = jnp.dot(o_ref[0], wo_ref[...], preferred_element_type=F32)
    x1 = x_ref[0] + gaa_ref[0] * a
    h = _norm_modulate(x1, g_ref[...], scf_ref[0], shf_ref[0]).astype(BF16)
    u = jnp.dot(h, wg_ref[...], preferred_element_type=F32)
    v = jnp.dot(h, wu_ref[...], preferred_element_type=F32)
    act = (u / (1.0 + jnp.exp(-u)) * v).astype(BF16)
    y = jnp.dot(act, wd_ref[...], preferred_element_type=F32)
    out_ref[0] = x1 + gaf_ref[0] * y


def _out_ffn(o, xx, mod3, g_ffn, w_out, w_gate, w_up, w_down, n_tiles, n_tok):
    nb = xx.shape[0]
    full = _resident_spec
    tile = pl.BlockSpec((1, TM, D_MODEL), lambda b, i: (b, i, 0))
    return pl.pallas_call(
        _ffn_kernel,
        out_shape=jax.ShapeDtypeStruct((nb, n_tok, D_MODEL), F32),
        grid=(nb, n_tiles),
        in_specs=[tile, tile, _mod_spec(2), _mod_spec(3), _mod_spec(4), _mod_spec(5),
                  full((1, D_MODEL)), full((D_MODEL, D_MODEL)),
                  full((D_MODEL, FFN_HIDDEN)), full((D_MODEL, FFN_HIDDEN)),
                  full((FFN_HIDDEN, D_MODEL))],
        out_specs=tile,
        compiler_params=pltpu.CompilerParams(
            dimension_semantics=("arbitrary", "arbitrary"), vmem_limit_bytes=VMEM_LIMIT),
        name="out_ffn",
    )(o, xx, mod3, mod3, mod3, mod3, g_ffn, w_out, w_gate, w_up, w_down)


def kernel(x, c, ctx, c_ctx, w_mod, b_mod, norm_attn, norm_ffn, w_in, qk_norm, sink, rpb,
           w_out, w_gate, w_up, w_down):
    depth = w_mod.shape[0]
    nb = x.shape[0]
    assert x.shape == (nb, SEQ, D_MODEL) and ctx.shape == (nb, CTX_LEN, D_MODEL) and nb <= CTX_ROW

    c8 = jnp.zeros((MOD_ROWS, D_MODEL), F32).at[:nb].set(c).at[CTX_ROW].set(c_ctx)
    mod = _modulation(c8, w_mod, b_mod)

    in_perm, out_perm = _in_perm(), _out_perm()
    w_in_p = jnp.take(w_in, in_perm, axis=2).astype(BF16)
    w_out_p = jnp.take(w_out, out_perm, axis=1).astype(BF16)
    w_gate_b, w_up_b, w_down_b = w_gate.astype(BF16), w_up.astype(BF16), w_down.astype(BF16)
    cos_t, sin_t = _rope_tables()
    head_id = np.arange(2 * LANE) // HEAD_DIM
    bd = jnp.asarray(head_id[:, None] == head_id[None, :], BF16)
    pats, tbl_idx = _nb_tables()
    flat = rpb.reshape(depth, B_HEADS, -1)
    bias_all = jnp.where(pats[None, None] >= 0, jnp.take(flat, np.maximum(pats, 0), axis=2), NEG_INF)
    bias_all = jnp.transpose(bias_all, (0, 2, 1, 3, 4))
    tbl_idx = jnp.asarray(tbl_idx)

    xx = jnp.concatenate([x, ctx], axis=1)
    for l in range(depth):
        last = l == depth - 1
        mod3 = mod[l].reshape(MOD_ROWS, 1, 6 * D_MODEL)
        qkv = _in_projection(xx, mod3, norm_attn[l][None], w_in_p[l], _gain_vector(qk_norm[l])[None],
                             bd, cos_t, sin_t)
        o = _attention(qkv, sink[l], bias_all[l], tbl_idx)
        if not last:
            o = _ctx_attention(qkv, sink[l], o)
        xx = _out_ffn(o, xx, mod3, norm_ffn[l][None], w_out_p[l], w_gate_b[l], w_up_b[l], w_down_b[l],
                      N_LAT_TILES if last else TOK // TM, SEQ if last else TOK)
    return xx
```

```python
import numpy as np
import jax
import jax.numpy as jnp
from jax import lax
from jax.experimental import pallas as pl
from jax.experimental.pallas import tpu as pltpu

D_MODEL = 1024
SEQ = 4096
CTX_LEN = 256
GRID_W = 64
ROWS = SEQ // GRID_W
HEAD_DIM = 64
A_HEADS = 6
B_HEADS = 4
C_HEADS = 6
GQA_G = 3
WINDOW = 128
NA_KH = 8
NA_KW = 16
FFN_HIDDEN = 2816
IN_WIDTH = 2048
ROPE_THETA = 10000.0
EPS = 1e-6
NEG_INF = -1e30
MOD_ROWS = 8
CTX_ROW = 4

LANE = 128
TM = 512
LAT_TILES = SEQ // TM
FFN_CHUNK = FFN_HIDDEN // 2
TQ = 128
NQ = SEQ // TQ
Q_ROWS = TQ // GRID_W
NB_ROWS = Q_ROWS + NA_KH
NB_KEYS = NB_ROWS * GRID_W
A_SPAN = TQ + 2 * WINDOW
C_CHUNK = 512
VMEM_LIMIT = 56 * 1024 * 1024

COL_QA, COL_QC, COL_KA, COL_KC = 0, 384, 768, 896
COL_QB, COL_KB = 1024, 1280
COL_VA, COL_VC, COL_VB = 1536, 1664, 1792
OCOL_A, OCOL_B, OCOL_C = 0, 384, 640

F32 = jnp.float32
BF16 = jnp.bfloat16
NT_DIMS = (((1,), (1,)), ((), ()))


def _gqa_cols(base):
    cols = np.empty(GQA_G * 2 * HEAD_DIM, np.int32)
    for j in range(GQA_G):
        for g in range(2):
            dst = j * LANE + g * HEAD_DIM
            cols[dst:dst + HEAD_DIM] = base + (g * GQA_G + j) * HEAD_DIM + np.arange(HEAD_DIM)
    return cols


def _in_perm():
    r = np.arange
    return np.concatenate([
        _gqa_cols(0),
        _gqa_cols(1408),
        384 + r(128),
        1792 + r(128),
        640 + r(256),
        896 + r(256),
        512 + r(128),
        1920 + r(128),
        1152 + r(256),
    ]).astype(np.int32)


def _out_perm():
    return np.concatenate([_gqa_cols(0), 384 + np.arange(256), _gqa_cols(640)]).astype(np.int32)


def _permute(w, perm, axis):
    cuts = [0] + [k for k in range(1, len(perm)) if perm[k] != perm[k - 1] + 1] + [len(perm)]
    parts = [lax.slice_in_dim(w, int(perm[a]), int(perm[b - 1]) + 1, axis=axis)
             for a, b in zip(cuts[:-1], cuts[1:])]
    return jnp.concatenate(parts, axis=axis)


def _gain_vector(qk):
    sc = HEAD_DIM ** -0.5
    t = jnp.tile
    return jnp.concatenate([
        t(qk[0] * sc, 6), t(qk[4] * sc, 6), t(qk[1], 2), t(qk[5], 2),
        t(qk[2] * sc, 4), t(qk[3], 4), jnp.ones((512,), F32)])


def _rope_tables():
    t = jnp.arange(SEQ, dtype=jnp.int32)
    row = (t // GRID_W).astype(F32)
    col = (t % GRID_W).astype(F32)
    n_freq = HEAD_DIM // 4
    inv = ROPE_THETA ** (-jnp.arange(n_freq, dtype=F32) / n_freq)
    ang = jnp.concatenate([row[:, None] * inv[None, :], col[:, None] * inv[None, :]], axis=-1)
    cos, sin = jnp.cos(ang), jnp.sin(ang)
    cos = jnp.concatenate([cos, jnp.ones((TM, HEAD_DIM // 2), F32)], axis=0)
    sin = jnp.concatenate([sin, jnp.zeros((TM, HEAD_DIM // 2), F32)], axis=0)
    cos_t = jnp.tile(cos, (1, 2 * LANE // HEAD_DIM))
    sin_t = jnp.tile(jnp.concatenate([-sin, sin], axis=-1), (1, LANE // HEAD_DIM))
    return cos_t, sin_t


def _nb_tables():
    pats = []
    for i in range(NQ):
        ks = int(np.clip(Q_ROWS * i - NA_KH // 2, 0, ROWS - NB_ROWS))
        qi = np.arange(TQ)
        r = Q_ROWS * i + qi // GRID_W
        c = qi % GRID_W
        kk = np.arange(NB_KEYS)
        krow = ks + kk // GRID_W
        kcol = kk % GRID_W
        rs = np.clip(r - NA_KH // 2, 0, ROWS - NA_KH)
        ws = np.clip(c - NA_KW // 2, 0, GRID_W - NA_KW)
        valid = ((krow[None, :] >= rs[:, None]) & (krow[None, :] < rs[:, None] + NA_KH)
                 & (kcol[None, :] >= ws[:, None]) & (kcol[None, :] < ws[:, None] + NA_KW))
        dr = np.clip(krow[None, :] - r[:, None] + (NA_KH - 1), 0, 2 * NA_KH - 2)
        dc = np.clip(kcol[None, :] - c[:, None] + (NA_KW - 1), 0, 2 * NA_KW - 2)
        pats.append(np.where(valid, dr * (2 * NA_KW - 1) + dc, -1).astype(np.int32))
    pats = np.stack(pats)
    uniq, inv = np.unique(pats.reshape(NQ, -1), axis=0, return_inverse=True)
    uniq = uniq.reshape(-1, TQ, NB_KEYS)
    return uniq, inv.reshape(-1).astype(np.int32)


def _compiler_params(n_axes):
    return pltpu.CompilerParams(dimension_semantics=("arbitrary",) * n_axes,
                                vmem_limit_bytes=VMEM_LIMIT)


def _mod_kernel(c_ref, w_ref, b_ref, o_ref):
    c = c_ref[...]
    a = (c / (1.0 + jnp.exp(-c))).astype(BF16)
    o_ref[0] = jnp.dot(a, w_ref[0].astype(BF16), preferred_element_type=F32) + b_ref[0]


def _modulation(c8, w_mod, b_mod):
    depth = w_mod.shape[0]
    tn = 2048
    return pl.pallas_call(
        _mod_kernel,
        out_shape=jax.ShapeDtypeStruct((depth, MOD_ROWS, 6 * D_MODEL), F32),
        grid=(depth, 6 * D_MODEL // tn),
        in_specs=[pl.BlockSpec((MOD_ROWS, D_MODEL), lambda l, n: (0, 0)),
                  pl.BlockSpec((1, D_MODEL, tn), lambda l, n: (l, 0, n)),
                  pl.BlockSpec((1, 1, tn), lambda l, n: (l, 0, n))],
        out_specs=pl.BlockSpec((1, MOD_ROWS, tn), lambda l, n: (l, 0, n)),
        compiler_params=_compiler_params(2),
        name="modulation",
    )(c8, w_mod, b_mod.reshape(depth, 1, 6 * D_MODEL))


def _mod_spec(chunk, n_lat_tiles):
    return pl.BlockSpec(
        (1, 1, D_MODEL),
        lambda i: (jnp.where(i < n_lat_tiles, i // LAT_TILES, CTX_ROW), 0, chunk))


def _resident_spec(shape):
    return pl.BlockSpec(shape, lambda i: (0,) * len(shape), pipeline_mode=pl.Buffered(1))


def _norm_modulate(x, g, scale, shift):
    ms = jnp.mean(x * x, axis=-1, keepdims=True)
    return x * lax.rsqrt(ms + EPS) * (g * (1.0 + scale)) + shift


def _inproj_kernel(x_ref, sh_ref, sc_ref, g_ref, w_ref, gv_ref, bd_ref, cos_ref, sin_ref, o_ref):
    h = _norm_modulate(x_ref[...], g_ref[...], sc_ref[0], sh_ref[0]).astype(BF16)
    lane = lax.broadcasted_iota(jnp.int32, (TM, LANE), 1)
    first_half = (lane % HEAD_DIM) < (HEAD_DIM // 2)
    cos = cos_ref[...]
    sin = sin_ref[...]
    cw = 2 * LANE
    for c in range(IN_WIDTH // cw):
        cs = slice(c * cw, (c + 1) * cw)
        p = jnp.dot(h, w_ref[:, cs], preferred_element_type=F32)
        if c * cw < COL_VA:
            ss = jnp.dot((p * p).astype(BF16), bd_ref[...], preferred_element_type=F32)
            p = p * lax.rsqrt(ss * (1.0 / HEAD_DIM) + EPS) * gv_ref[:, cs]
        if c * cw < COL_QB:
            halves = []
            for s in range(2):
                xh = p[:, s * LANE:(s + 1) * LANE]
                partner = jnp.where(first_half,
                                    pltpu.roll(xh, LANE - HEAD_DIM // 2, 1),
                                    pltpu.roll(xh, HEAD_DIM // 2, 1))
                halves.append(xh * cos + partner * sin)
            p = jnp.concatenate(halves, axis=1)
        o_ref[:, cs] = p.astype(BF16)


def _in_projection(xx, mod3, g_attn, w_in, gvec, bd, cos_t, sin_t, n_lat_tiles):
    ntok = xx.shape[0]
    rope_spec = pl.BlockSpec(
        (TM, LANE), lambda i: (jnp.where(i < n_lat_tiles, i % LAT_TILES, LAT_TILES), 0))
    return pl.pallas_call(
        _inproj_kernel,
        out_shape=jax.ShapeDtypeStruct((ntok, IN_WIDTH), BF16),
        grid=(ntok // TM,),
        in_specs=[pl.BlockSpec((TM, D_MODEL), lambda i: (i, 0)),
                  _mod_spec(0, n_lat_tiles), _mod_spec(1, n_lat_tiles),
                  _resident_spec((1, D_MODEL)), _resident_spec((D_MODEL, IN_WIDTH)),
                  _resident_spec((1, IN_WIDTH)), _resident_spec((2 * LANE, 2 * LANE)),
                  rope_spec, rope_spec],
        out_specs=pl.BlockSpec((TM, IN_WIDTH), lambda i: (i, 0)),
        compiler_params=_compiler_params(1),
        name="in_projection",
    )(xx, mod3, mod3, g_attn, w_in, gvec, bd, cos_t, sin_t)


def _stack_gqa(q_ref, nq):
    low = lax.broadcasted_iota(jnp.int32, (nq, LANE), 1) < HEAD_DIM
    blocks = []
    for g in range(2):
        for j in range(GQA_G):
            qj = q_ref[:, j * LANE:(j + 1) * LANE]
            blocks.append(jnp.where(low if g == 0 else jnp.logical_not(low), qj, jnp.zeros_like(qj)))
    return jnp.concatenate(blocks, axis=0)


def _unstack_gqa(o, nq):
    low = lax.broadcasted_iota(jnp.int32, (nq, LANE), 1) < HEAD_DIM
    return [jnp.where(low, o[j * nq:(j + 1) * nq], o[(GQA_G + j) * nq:(GQA_G + j + 1) * nq])
            for j in range(GQA_G)]


def _stack_mha(q_ref, nq):
    q = q_ref[...]
    head = lax.broadcasted_iota(jnp.int32, (nq, B_HEADS * HEAD_DIM), 1) // HEAD_DIM
    return jnp.concatenate([jnp.where(head == h, q, jnp.zeros_like(q)) for h in range(B_HEADS)], axis=0)


def _unstack_mha(o, nq):
    head = lax.broadcasted_iota(jnp.int32, (nq, B_HEADS * HEAD_DIM), 1) // HEAD_DIM
    out = jnp.where(head == 0, o[0:nq], 0.0)
    for h in range(1, B_HEADS):
        out = jnp.where(head == h, o[h * nq:(h + 1) * nq], out)
    return out


def _qk(q, k):
    return lax.dot_general(q, k, NT_DIMS, preferred_element_type=F32)


def _softmax_pv(parts, extra_logit=None):
    m = None
    for s, _ in parts:
        sm = jnp.max(s, axis=-1, keepdims=True)
        m = sm if m is None else jnp.maximum(m, sm)
    if extra_logit is not None:
        m = jnp.maximum(m, extra_logit)
    den = None if extra_logit is None else jnp.exp(extra_logit - m)
    acc = None
    for s, v in parts:
        p = jnp.exp(s - m)
        ps = jnp.sum(p, axis=-1, keepdims=True)
        den = ps if den is None else den + ps
        pv = jnp.dot(p.astype(BF16), v, preferred_element_type=F32)
        acc = pv if acc is None else acc + pv
    return acc * (1.0 / den)


def _sink_column(sink_ref, nq):
    blk = lax.broadcasted_iota(jnp.int32, (A_HEADS * nq, 1), 0) // nq
    col = jnp.full((A_HEADS * nq, 1), sink_ref[0], F32)
    for n in range(1, A_HEADS):
        col = jnp.where(blk == n, sink_ref[n], col)
    return col


def _global_attention(q, k_ref, v_ref, kc_ref, vc_ref):
    chunks = [(k_ref, v_ref, slice(k0, k0 + C_CHUNK)) for k0 in range(0, SEQ, C_CHUNK)]
    chunks.append((kc_ref, vc_ref, slice(0, CTX_LEN)))
    m = l = acc = None
    for kr, vr, ks in chunks:
        s = _qk(q, kr[ks, :])
        sm = jnp.max(s, axis=-1, keepdims=True)
        if m is None:
            m_new = sm
        else:
            m_new = jnp.maximum(m, sm)
            alpha = jnp.exp(m - m_new)
        p = jnp.exp(s - m_new)
        ps = jnp.sum(p, axis=-1, keepdims=True)
        pv = jnp.dot(p.astype(BF16), vr[ks, :], preferred_element_type=F32)
        if m is None:
            l, acc = ps, pv
        else:
            l = alpha * l + ps
            acc = alpha * acc + pv
        m = m_new
    return acc * (1.0 / l)


def _attn_kernel(tbl_ref, sink_ref, qa_ref, qc_ref, qb_ref,
                 ka_ref, va_ref, kc_ref, vc_ref, kb_ref, vb_ref,
                 kax_ref, vax_ref, kcx_ref, vcx_ref, kbx_ref, vbx_ref, bias_ref, o_ref):
    del tbl_ref
    i = pl.program_id(1)

    qa = _stack_gqa(qa_ref, TQ)
    start = pl.multiple_of(jnp.clip(i * TQ - WINDOW, 0, SEQ - A_SPAN), LANE)
    qpos = i * TQ + lax.broadcasted_iota(jnp.int32, (TQ, A_SPAN), 0)
    kpos = start + lax.broadcasted_iota(jnp.int32, (TQ, A_SPAN), 1)
    valid = jnp.abs(qpos - kpos) <= WINDOW
    s_loc = _qk(qa, ka_ref[pl.ds(start, A_SPAN), :]).reshape(A_HEADS, TQ, A_SPAN)
    s_loc = jnp.where(valid[None], s_loc, NEG_INF).reshape(A_HEADS * TQ, A_SPAN)
    s_ctx = _qk(qa, kax_ref[...])
    oa = _softmax_pv([(s_loc, va_ref[pl.ds(start, A_SPAN), :]), (s_ctx, vax_ref[...])],
                     _sink_column(sink_ref, TQ))
    for j, blk in enumerate(_unstack_gqa(oa, TQ)):
        o_ref[:, OCOL_A + j * LANE:OCOL_A + (j + 1) * LANE] = blk.astype(BF16)

    qb = _stack_mha(qb_ref, TQ)
    krow0 = jnp.clip(Q_ROWS * i - NA_KH // 2, 0, ROWS - NB_ROWS)
    kstart = pl.multiple_of(krow0 * GRID_W, GRID_W)
    s_loc = _qk(qb, kb_ref[pl.ds(kstart, NB_KEYS), :]) + bias_ref[0].reshape(B_HEADS * TQ, NB_KEYS)
    s_ctx = _qk(qb, kbx_ref[...])
    ob = _softmax_pv([(s_loc, vb_ref[pl.ds(kstart, NB_KEYS), :]), (s_ctx, vbx_ref[...])])
    o_ref[:, OCOL_B:OCOL_C] = _unstack_mha(ob, TQ).astype(BF16)

    oc = _global_attention(_stack_gqa(qc_ref, TQ), kc_ref, vc_ref, kcx_ref, vcx_ref)
    for j, blk in enumerate(_unstack_gqa(oc, TQ)):
        o_ref[:, OCOL_C + j * LANE:OCOL_C + (j + 1) * LANE] = blk.astype(BF16)


def _kv_cols():
    return [(128, COL_KA // 128), (128, COL_VA // 128), (128, COL_KC // 128), (128, COL_VC // 128),
            (256, COL_KB // 256), (256, COL_VB // 256)]


def _attention(qkv, sink, bias_tbl, tbl_idx, nb):
    ntok = qkv.shape[0]
    ctx_blk0 = nb * SEQ // CTX_LEN
    qspec = lambda w, cb: pl.BlockSpec((TQ, w), lambda b, i, t: (b * NQ + i, cb))
    lat = lambda w, cb: pl.BlockSpec((SEQ, w), lambda b, i, t: (b, cb))
    cxt = lambda w, cb: pl.BlockSpec((CTX_LEN, w), lambda b, i, t: (ctx_blk0 + b, cb))
    grid_spec = pltpu.PrefetchScalarGridSpec(
        num_scalar_prefetch=1,
        grid=(nb, NQ),
        in_specs=[pl.BlockSpec(memory_space=pltpu.SMEM),
                  qspec(384, COL_QA // 384), qspec(384, COL_QC // 384), qspec(256, COL_QB // 256)]
                 + [lat(w, cb) for w, cb in _kv_cols()] + [cxt(w, cb) for w, cb in _kv_cols()]
                 + [pl.BlockSpec((1, B_HEADS, TQ, NB_KEYS), lambda b, i, t: (t[i], 0, 0, 0))],
        out_specs=pl.BlockSpec((TQ, D_MODEL), lambda b, i, t: (b * NQ + i, 0)),
    )
    return pl.pallas_call(
        _attn_kernel,
        out_shape=jax.ShapeDtypeStruct((ntok, D_MODEL), BF16),
        grid_spec=grid_spec,
        compiler_params=_compiler_params(2),
        name="attention",
    )(tbl_idx, sink, *([qkv] * 15), bias_tbl)


def _ctx_attn_kernel(sink_ref, qa_ref, qc_ref, qb_ref, ka_ref, va_ref, kc_ref, vc_ref,
                     kb_ref, vb_ref, prev_ref, o_ref):
    del prev_ref
    n = CTX_LEN
    qa = _stack_gqa(qa_ref, n)
    oa = _softmax_pv([(_qk(qa, ka_ref[...]), va_ref[...])], _sink_column(sink_ref, n))
    for j, blk in enumerate(_unstack_gqa(oa, n)):
        o_ref[:, OCOL_A + j * LANE:OCOL_A + (j + 1) * LANE] = blk.astype(BF16)
    qb = _stack_mha(qb_ref, n)
    ob = _softmax_pv([(_qk(qb, kb_ref[...]), vb_ref[...])])
    o_ref[:, OCOL_B:OCOL_C] = _unstack_mha(ob, n).astype(BF16)
    qc = _stack_gqa(qc_ref, n)
    oc = _softmax_pv([(_qk(qc, kc_ref[...]), vc_ref[...])])
    for j, blk in enumerate(_unstack_gqa(oc, n)):
        o_ref[:, OCOL_C + j * LANE:OCOL_C + (j + 1) * LANE] = blk.astype(BF16)


def _ctx_attention(qkv, sink, o_prev, nb):
    ntok = qkv.shape[0]
    ctx_blk0 = nb * SEQ // CTX_LEN
    spec = lambda w, cb: pl.BlockSpec((CTX_LEN, w), lambda b: (ctx_blk0 + b, cb))
    return pl.pallas_call(
        _ctx_attn_kernel,
        out_shape=jax.ShapeDtypeStruct((ntok, D_MODEL), BF16),
        grid=(nb,),
        in_specs=[pl.BlockSpec(memory_space=pltpu.SMEM),
                  spec(384, COL_QA // 384), spec(384, COL_QC // 384), spec(256, COL_QB // 256)]
                 + [spec(w, cb) for w, cb in _kv_cols()]
                 + [pl.BlockSpec(memory_space=pl.ANY)],
        out_specs=pl.BlockSpec((CTX_LEN, D_MODEL), lambda b: (ctx_blk0 + b, 0)),
        input_output_aliases={10: 0},
        compiler_params=_compiler_params(1),
        name="ctx_attention",
    )(sink, *([qkv] * 9), o_prev)


def _ffn_kernel(o_ref, x_ref, gaa_ref, shf_ref, scf_ref, gaf_ref, g_ref,
                wo_ref, wg_ref, wu_ref, wd_ref, out_ref):
    a = jnp.dot(o_ref[...], wo_ref[...], preferred_element_type=F32)
    x1 = x_ref[...] + gaa_ref[0] * a
    h = _norm_modulate(x1, g_ref[...], scf_ref[0], shf_ref[0]).astype(BF16)
    y = None
    for c in range(FFN_HIDDEN // FFN_CHUNK):
        cs = slice(c * FFN_CHUNK, (c + 1) * FFN_CHUNK)
        u = jnp.dot(h, wg_ref[:, cs], preferred_element_type=F32)
        v = jnp.dot(h, wu_ref[:, cs], preferred_element_type=F32)
        act = (u / (1.0 + jnp.exp(-u)) * v).astype(BF16)
        yc = jnp.dot(act, wd_ref[cs, :], preferred_element_type=F32)
        y = yc if y is None else y + yc
    out_ref[...] = x1 + gaf_ref[0] * y


def _out_ffn(o, xx, mod3, g_ffn, w_out, w_gate, w_up, w_down, n_tiles, n_lat_tiles):
    tile = pl.BlockSpec((TM, D_MODEL), lambda i: (i, 0))
    return pl.pallas_call(
        _ffn_kernel,
        out_shape=jax.ShapeDtypeStruct((n_tiles * TM, D_MODEL), F32),
        grid=(n_tiles,),
        in_specs=[tile, tile] + [_mod_spec(k, n_lat_tiles) for k in (2, 3, 4, 5)]
                 + [_resident_spec((1, D_MODEL)), _resident_spec((D_MODEL, D_MODEL)),
                    _resident_spec((D_MODEL, FFN_HIDDEN)), _resident_spec((D_MODEL, FFN_HIDDEN)),
                    _resident_spec((FFN_HIDDEN, D_MODEL))],
        out_specs=tile,
        compiler_params=_compiler_params(1),
        name="out_ffn",
    )(o, xx, mod3, mod3, mod3, mod3, g_ffn, w_out, w_gate, w_up, w_down)


def kernel(x, c, ctx, c_ctx, w_mod, b_mod, norm_attn, norm_ffn, w_in, qk_norm, sink, rpb,
           w_out, w_gate, w_up, w_down):
    depth = w_mod.shape[0]
    nb = x.shape[0]
    assert x.shape == (nb, SEQ, D_MODEL) and ctx.shape == (nb, CTX_LEN, D_MODEL) and nb <= CTX_ROW
    assert (nb * CTX_LEN) % TM == 0
    n_lat_tiles = nb * LAT_TILES
    n_tiles = n_lat_tiles + nb * CTX_LEN // TM

    c8 = jnp.zeros((MOD_ROWS, D_MODEL), F32).at[:nb].set(c).at[CTX_ROW].set(c_ctx)
    mod = _modulation(c8, w_mod, b_mod)

    w_in_p = _permute(w_in, _in_perm(), 2).astype(BF16)
    w_out_p = _permute(w_out, _out_perm(), 1).astype(BF16)
    w_gate_b, w_up_b, w_down_b = w_gate.astype(BF16), w_up.astype(BF16), w_down.astype(BF16)
    cos_t, sin_t = _rope_tables()
    head_id = np.arange(2 * LANE) // HEAD_DIM
    bd = jnp.asarray(head_id[:, None] == head_id[None, :], BF16)
    pats, tbl_idx = _nb_tables()
    flat = rpb.reshape(depth, B_HEADS, -1)
    bias_all = jnp.where(pats[None, None] >= 0, jnp.take(flat, np.maximum(pats, 0), axis=2), NEG_INF)
    bias_all = jnp.transpose(bias_all, (0, 2, 1, 3, 4))
    tbl_idx = jnp.asarray(tbl_idx)

    xx = jnp.concatenate([x.reshape(nb * SEQ, D_MODEL), ctx.reshape(nb * CTX_LEN, D_MODEL)], axis=0)
    for l in range(depth):
        last = l == depth - 1
        mod3 = mod[l].reshape(MOD_ROWS, 1, 6 * D_MODEL)
        qkv = _in_projection(xx, mod3, norm_attn[l][None], w_in_p[l], _gain_vector(qk_norm[l])[None],
                             bd, cos_t, sin_t, n_lat_tiles)
        o = _attention(qkv, sink[l], bias_all[l], tbl_idx, nb)
        if not last:
            o = _ctx_attention(qkv, sink[l], o, nb)
        xx = _out_ffn(o, xx, mod3, norm_ffn[l][None], w_out_p[l], w_gate_b[l], w_up_b[l], w_down_b[l],
                      n_lat_tiles if last else n_tiles, n_lat_tiles)
    return xx.reshape(nb, SEQ, D_MODEL)
```

```python
import numpy as np
import jax
import jax.numpy as jnp
from jax import lax
from jax.experimental import pallas as pl
from jax.experimental.pallas import tpu as pltpu

D_MODEL = 1024
SEQ = 4096
CTX_LEN = 256
GRID_W = 64
ROWS = SEQ // GRID_W
HEAD_DIM = 64
A_HEADS = 6
B_HEADS = 4
C_HEADS = 6
GQA_G = 3
WINDOW = 128
NA_KH = 8
NA_KW = 16
FFN_HIDDEN = 2816
IN_WIDTH = 2048
ROPE_THETA = 10000.0
EPS = 1e-6
NEG_INF = -1e30
LOG2E = 1.4426950408889634
MOD_ROWS = 8
CTX_ROW = 4

LANE = 128
TM = 512
LAT_TILES = SEQ // TM
FFN_CHUNK = FFN_HIDDEN // 2
TQ = 128
NQ = SEQ // TQ
Q_ROWS = TQ // GRID_W
NB_ROWS = Q_ROWS + NA_KH
NB_KEYS = NB_ROWS * GRID_W
A_SPAN = TQ + 2 * WINDOW
C_CHUNK = 1024
PIECE = 128
VMEM_LIMIT = 56 * 1024 * 1024

COL_QA, COL_QC, COL_KA, COL_KC = 0, 384, 768, 896
COL_QB, COL_KB = 1024, 1280
COL_VA, COL_VC, COL_VB = 1536, 1664, 1792
OCOL_A, OCOL_B, OCOL_C = 0, 384, 640

F32 = jnp.float32
BF16 = jnp.bfloat16
NT_DIMS = (((1,), (1,)), ((), ()))


def _gqa_cols(base):
    cols = np.empty(GQA_G * 2 * HEAD_DIM, np.int32)
    for j in range(GQA_G):
        for g in range(2):
            dst = j * LANE + g * HEAD_DIM
            cols[dst:dst + HEAD_DIM] = base + (g * GQA_G + j) * HEAD_DIM + np.arange(HEAD_DIM)
    return cols


def _in_perm():
    r = np.arange
    return np.concatenate([
        _gqa_cols(0),
        _gqa_cols(1408),
        384 + r(128),
        1792 + r(128),
        640 + r(256),
        896 + r(256),
        512 + r(128),
        1920 + r(128),
        1152 + r(256),
    ]).astype(np.int32)


def _out_perm():
    return np.concatenate([_gqa_cols(0), 384 + np.arange(256), _gqa_cols(640)]).astype(np.int32)


def _permute(w, perm, axis):
    cuts = [0] + [k for k in range(1, len(perm)) if perm[k] != perm[k - 1] + 1] + [len(perm)]
    parts = [lax.slice_in_dim(w, int(perm[a]), int(perm[b - 1]) + 1, axis=axis)
             for a, b in zip(cuts[:-1], cuts[1:])]
    return jnp.concatenate(parts, axis=axis)


def _gain_vector(qk):
    sc = HEAD_DIM ** -0.5 * LOG2E
    t = jnp.tile
    return jnp.concatenate([
        t(qk[0] * sc, 6), t(qk[4] * sc, 6), t(qk[1], 2), t(qk[5], 2),
        t(qk[2] * sc, 4), t(qk[3], 4), jnp.ones((512,), F32)])


def _rope_tables():
    t = jnp.arange(SEQ, dtype=jnp.int32)
    row = (t // GRID_W).astype(F32)
    col = (t % GRID_W).astype(F32)
    n_freq = HEAD_DIM // 4
    inv = ROPE_THETA ** (-jnp.arange(n_freq, dtype=F32) / n_freq)
    ang = jnp.concatenate([row[:, None] * inv[None, :], col[:, None] * inv[None, :]], axis=-1)
    cos, sin = jnp.cos(ang), jnp.sin(ang)
    cos = jnp.concatenate([cos, jnp.ones((TM, HEAD_DIM // 2), F32)], axis=0)
    sin = jnp.concatenate([sin, jnp.zeros((TM, HEAD_DIM // 2), F32)], axis=0)
    cos_t = jnp.tile(cos, (1, 2 * LANE // HEAD_DIM))
    sin_t = jnp.tile(jnp.concatenate([-sin, sin], axis=-1), (1, LANE // HEAD_DIM))
    return cos_t, sin_t


def _nb_row_classes():
    pats = []
    for i in range(NQ):
        ks = int(np.clip(Q_ROWS * i - NA_KH // 2, 0, ROWS - NB_ROWS))
        r = Q_ROWS * i + np.arange(Q_ROWS)[:, None]
        krow = ks + np.arange(NB_ROWS)[None, :]
        rs = np.clip(r - NA_KH // 2, 0, ROWS - NA_KH)
        valid = (krow >= rs) & (krow < rs + NA_KH)
        pats.append(np.where(valid, krow - r + (NA_KH - 1), -1))
    uniq, inv = np.unique(np.stack(pats).reshape(NQ, -1), axis=0, return_inverse=True)
    return uniq.reshape(-1, Q_ROWS, NB_ROWS), inv.reshape(-1).astype(np.int32)


def _bias_tables(rpb, row_classes):
    w = GRID_W
    pad = jnp.pad(rpb, ((0, 0), (0, 0), (0, 0), (w, w)), constant_values=NEG_INF)
    cols = jnp.stack([lax.slice_in_dim(pad, w + NA_KW - 1 - c, 2 * w + NA_KW - 1 - c, axis=3)
                      for c in range(w)], axis=3)
    c = np.arange(w)
    ws = np.clip(c - NA_KW // 2, 0, w - NA_KW)
    col_ok = (c[None, :] >= ws[:, None]) & (c[None, :] < ws[:, None] + NA_KW)
    cols = jnp.where(col_ok, cols, NEG_INF)
    neg = jnp.full(cols.shape[:2] + (w, w), NEG_INF, F32)
    tables = []
    for cls in row_classes:
        rows = [jnp.concatenate([cols[:, :, int(d)] if d >= 0 else neg for d in cls[a]], axis=-1)
                for a in range(Q_ROWS)]
        tables.append(jnp.concatenate(rows, axis=-2))
    return jnp.stack(tables, axis=1)


def _compiler_params(n_axes, flags=None):
    return pltpu.CompilerParams(dimension_semantics=("arbitrary",) * n_axes,
                                vmem_limit_bytes=VMEM_LIMIT, flags=flags)


def _mod_kernel(c_ref, w_ref, b_ref, o_ref):
    c = c_ref[...]
    a = (c / (1.0 + jnp.exp(-c))).astype(BF16)
    o_ref[0] = jnp.dot(a, w_ref[0].astype(BF16), preferred_element_type=F32) + b_ref[0]


def _modulation(c8, w_mod, b_mod):
    depth = w_mod.shape[0]
    tn = 2048
    return pl.pallas_call(
        _mod_kernel,
        out_shape=jax.ShapeDtypeStruct((depth, MOD_ROWS, 6 * D_MODEL), F32),
        grid=(depth, 6 * D_MODEL // tn),
        in_specs=[pl.BlockSpec((MOD_ROWS, D_MODEL), lambda l, n: (0, 0)),
                  pl.BlockSpec((1, D_MODEL, tn), lambda l, n: (l, 0, n)),
                  pl.BlockSpec((1, 1, tn), lambda l, n: (l, 0, n))],
        out_specs=pl.BlockSpec((1, MOD_ROWS, tn), lambda l, n: (l, 0, n)),
        compiler_params=_compiler_params(2),
        name="modulation",
    )(c8, w_mod, b_mod.reshape(depth, 1, 6 * D_MODEL))


def _mod_spec(chunk, n_lat_tiles):
    return pl.BlockSpec(
        (1, 1, D_MODEL),
        lambda i: (jnp.where(i < n_lat_tiles, i // LAT_TILES, CTX_ROW), 0, chunk))


def _resident_spec(shape):
    return pl.BlockSpec(shape, lambda i: (0,) * len(shape), pipeline_mode=pl.Buffered(1))


def _norm_modulate(x, g, scale, shift):
    ms = jnp.mean(x * x, axis=-1, keepdims=True)
    return x * lax.rsqrt(ms + EPS) * (g * (1.0 + scale)) + shift


def _inproj_kernel(x_ref, sh_ref, sc_ref, g_ref, w_ref, gv_ref, bd_ref, cos_ref, sin_ref, o_ref):
    h = _norm_modulate(x_ref[...], g_ref[...], sc_ref[0], sh_ref[0]).astype(BF16)
    lane = lax.broadcasted_iota(jnp.int32, (TM, LANE), 1)
    first_half = (lane % HEAD_DIM) < (HEAD_DIM // 2)
    cos = cos_ref[...]
    sin = sin_ref[...]
    cw = 2 * LANE
    for c in range(IN_WIDTH // cw):
        cs = slice(c * cw, (c + 1) * cw)
        p = jnp.dot(h, w_ref[:, cs], preferred_element_type=F32)
        if c * cw < COL_VA:
            ss = jnp.dot((p * p).astype(BF16), bd_ref[...], preferred_element_type=F32)
            p = p * lax.rsqrt(ss * (1.0 / HEAD_DIM) + EPS) * gv_ref[:, cs]
        if c * cw < COL_QB:
            halves = []
            for s in range(2):
                xh = p[:, s * LANE:(s + 1) * LANE]
                partner = jnp.where(first_half,
                                    pltpu.roll(xh, LANE - HEAD_DIM // 2, 1),
                                    pltpu.roll(xh, HEAD_DIM // 2, 1))
                halves.append(xh * cos + partner * sin)
            p = jnp.concatenate(halves, axis=1)
        o_ref[:, cs] = p.astype(BF16)


def _in_projection(xx, mod3, g_attn, w_in, gvec, bd, cos_t, sin_t, n_lat_tiles):
    ntok = xx.shape[0]
    rope_spec = pl.BlockSpec(
        (TM, LANE), lambda i: (jnp.where(i < n_lat_tiles, i % LAT_TILES, LAT_TILES), 0))
    return pl.pallas_call(
        _inproj_kernel,
        out_shape=jax.ShapeDtypeStruct((ntok, IN_WIDTH), BF16),
        grid=(ntok // TM,),
        in_specs=[pl.BlockSpec((TM, D_MODEL), lambda i: (i, 0)),
                  _mod_spec(0, n_lat_tiles), _mod_spec(1, n_lat_tiles),
                  _resident_spec((1, D_MODEL)), _resident_spec((D_MODEL, IN_WIDTH)),
                  _resident_spec((1, IN_WIDTH)), _resident_spec((2 * LANE, 2 * LANE)),
                  rope_spec, rope_spec],
        out_specs=pl.BlockSpec((TM, IN_WIDTH), lambda i: (i, 0)),
        compiler_params=_compiler_params(1),
        name="in_projection",
    )(xx, mod3, mod3, g_attn, w_in, gvec, bd, cos_t, sin_t)


def _stack_gqa(q_ref, nq):
    low = lax.broadcasted_iota(jnp.int32, (nq, LANE), 1) < HEAD_DIM
    blocks = []
    for g in range(2):
        for j in range(GQA_G):
            qj = q_ref[:, j * LANE:(j + 1) * LANE]
            blocks.append(jnp.where(low if g == 0 else jnp.logical_not(low), qj, jnp.zeros_like(qj)))
    return jnp.concatenate(blocks, axis=0)


def _unstack_gqa(o, nq):
    low = lax.broadcasted_iota(jnp.int32, (nq, LANE), 1) < HEAD_DIM
    return [jnp.where(low, o[j * nq:(j + 1) * nq], o[(GQA_G + j) * nq:(GQA_G + j + 1) * nq])
            for j in range(GQA_G)]


def _stack_mha(q_ref, nq):
    q = q_ref[...]
    head = lax.broadcasted_iota(jnp.int32, (nq, B_HEADS * HEAD_DIM), 1) // HEAD_DIM
    return jnp.concatenate([jnp.where(head == h, q, jnp.zeros_like(q)) for h in range(B_HEADS)], axis=0)


def _unstack_mha(o, nq):
    head = lax.broadcasted_iota(jnp.int32, (nq, B_HEADS * HEAD_DIM), 1) // HEAD_DIM
    out = jnp.where(head == 0, o[0:nq], 0.0)
    for h in range(1, B_HEADS):
        out = jnp.where(head == h, o[h * nq:(h + 1) * nq], out)
    return out


def _qk(q, k):
    return lax.dot_general(q, k, NT_DIMS, preferred_element_type=F32)


def _softmax_pv(parts, nq, sink_ref=None, adjust=None):
    rows = parts[0][0].shape[0]
    width = parts[0][1].shape[1]
    mxu_sums = width == LANE
    probs = [[] for _ in parts]
    dens = []
    for r0 in range(0, rows, PIECE):
        sl = [s[r0:r0 + PIECE] for s, _ in parts]
        if adjust is not None:
            sl[0] = adjust(sl[0], r0 // nq, r0 % nq)
        m = None
        for sp in sl:
            sm = jnp.max(sp, axis=-1, keepdims=True)
            m = sm if m is None else jnp.maximum(m, sm)
        den = None
        if sink_ref is not None:
            sink = sink_ref[r0 // nq] * LOG2E
            m = jnp.maximum(m, sink)
            den = jnp.exp2(sink - m)
        for k, sp in enumerate(sl):
            p = jnp.exp2(sp - m)
            if not mxu_sums:
                ps = jnp.sum(p, axis=-1, keepdims=True)
                den = ps if den is None else den + ps
            probs[k].append(p.astype(BF16))
        if den is not None:
            dens.append(den)
    acc = None
    for k, (_, v) in enumerate(parts):
        if mxu_sums:
            v = jnp.concatenate([v, jnp.ones(v.shape, BF16)], axis=1)
        pv = jnp.dot(jnp.concatenate(probs[k], axis=0), v, preferred_element_type=F32)
        acc = pv if acc is None else acc + pv
    if not mxu_sums:
        return acc * (1.0 / jnp.concatenate(dens, axis=0))
    den = acc[:, width:]
    if dens:
        den = den + jnp.concatenate(dens, axis=0)
    return acc[:, :width] * (1.0 / den)


def _global_attention(q, k_ref, v_ref, kx_ref, vx_ref):
    chunks = [(k_ref, v_ref, slice(k0, k0 + C_CHUNK)) for k0 in range(0, SEQ, C_CHUNK)]
    chunks.append((kx_ref, vx_ref, slice(0, CTX_LEN)))
    n_pieces = q.shape[0] // PIECE
    m = [None] * n_pieces
    acc = None
    for kr, vr, ks in chunks:
        s = _qk(q, kr[ks, :])
        v_ext = jnp.concatenate([vr[ks, :], jnp.ones((ks.stop - ks.start, LANE), BF16)], axis=1)
        probs, alphas = [], []
        for t in range(n_pieces):
            sp = s[t * PIECE:(t + 1) * PIECE]
            sm = jnp.max(sp, axis=-1, keepdims=True)
            if acc is None:
                m_new = sm
            else:
                m_new = jnp.maximum(m[t], sm)
                alphas.append(jnp.exp2(m[t] - m_new))
            m[t] = m_new
            probs.append(jnp.exp2(sp - m_new).astype(BF16))
        pv = jnp.dot(jnp.concatenate(probs, axis=0), v_ext, preferred_element_type=F32)
        acc = pv if acc is None else acc * jnp.concatenate(alphas, axis=0) + pv
    return acc[:, :LANE] * (1.0 / acc[:, LANE:])


def _attn_kernel(tbl_ref, sink_ref, qa_ref, qc_ref, qb_ref,
                 ka_ref, va_ref, kc_ref, vc_ref, kb_ref, vb_ref,
                 kax_ref, vax_ref, kcx_ref, vcx_ref, kbx_ref, vbx_ref, bias_ref, o_ref):
    del tbl_ref
    i = pl.program_id(1)

    qa = _stack_gqa(qa_ref, TQ)
    start = pl.multiple_of(jnp.clip(i * TQ - WINDOW, 0, SEQ - A_SPAN), LANE)
    qpos = i * TQ + lax.broadcasted_iota(jnp.int32, (TQ, A_SPAN), 0)
    kpos = start + lax.broadcasted_iota(jnp.int32, (TQ, A_SPAN), 1)
    valid = jnp.abs(qpos - kpos) <= WINDOW
    s_loc = _qk(qa, ka_ref[pl.ds(start, A_SPAN), :])
    s_ctx = _qk(qa, kax_ref[...])
    oa = _softmax_pv([(s_loc, va_ref[pl.ds(start, A_SPAN), :]), (s_ctx, vax_ref[...])], TQ,
                     sink_ref=sink_ref,
                     adjust=lambda sp, n, q0: jnp.where(valid[q0:q0 + PIECE], sp, NEG_INF))
    for j, blk in enumerate(_unstack_gqa(oa, TQ)):
        o_ref[:, OCOL_A + j * LANE:OCOL_A + (j + 1) * LANE] = blk.astype(BF16)

    qb = _stack_mha(qb_ref, TQ)
    krow0 = jnp.clip(Q_ROWS * i - NA_KH // 2, 0, ROWS - NB_ROWS)
    kstart = pl.multiple_of(krow0 * GRID_W, GRID_W)
    s_loc = _qk(qb, kb_ref[pl.ds(kstart, NB_KEYS), :])
    s_ctx = _qk(qb, kbx_ref[...])
    ob = _softmax_pv([(s_loc, vb_ref[pl.ds(kstart, NB_KEYS), :]), (s_ctx, vbx_ref[...])], TQ,
                     adjust=lambda sp, n, q0: sp + bias_ref[0, n, q0:q0 + PIECE, :])
    o_ref[:, OCOL_B:OCOL_C] = _unstack_mha(ob, TQ).astype(BF16)

    oc = _global_attention(_stack_gqa(qc_ref, TQ), kc_ref, vc_ref, kcx_ref, vcx_ref)
    for j, blk in enumerate(_unstack_gqa(oc, TQ)):
        o_ref[:, OCOL_C + j * LANE:OCOL_C + (j + 1) * LANE] = blk.astype(BF16)


def _kv_cols():
    return [(128, COL_KA // 128), (128, COL_VA // 128), (128, COL_KC // 128), (128, COL_VC // 128),
            (256, COL_KB // 256), (256, COL_VB // 256)]


def _attention(qkv, sink, bias_tbl, tbl_idx, nb):
    ntok = qkv.shape[0]
    ctx_blk0 = nb * SEQ // CTX_LEN
    qspec = lambda w, cb: pl.BlockSpec((TQ, w), lambda b, i, t: (b * NQ + i, cb))
    lat = lambda w, cb: pl.BlockSpec((SEQ, w), lambda b, i, t: (b, cb))
    cxt = lambda w, cb: pl.BlockSpec((CTX_LEN, w), lambda b, i, t: (ctx_blk0 + b, cb))
    grid_spec = pltpu.PrefetchScalarGridSpec(
        num_scalar_prefetch=1,
        grid=(nb, NQ),
        in_specs=[pl.BlockSpec(memory_space=pltpu.SMEM),
                  qspec(384, COL_QA // 384), qspec(384, COL_QC // 384), qspec(256, COL_QB // 256)]
                 + [lat(w, cb) for w, cb in _kv_cols()] + [cxt(w, cb) for w, cb in _kv_cols()]
                 + [pl.BlockSpec((1, B_HEADS, TQ, NB_KEYS), lambda b, i, t: (t[i], 0, 0, 0))],
        out_specs=pl.BlockSpec((TQ, D_MODEL), lambda b, i, t: (b * NQ + i, 0)),
    )
    return pl.pallas_call(
        _attn_kernel,
        out_shape=jax.ShapeDtypeStruct((ntok, D_MODEL), BF16),
        grid_spec=grid_spec,
        compiler_params=_compiler_params(2),
        name="attention",
    )(tbl_idx, sink, *([qkv] * 15), bias_tbl)


def _ctx_attn_kernel(sink_ref, qa_ref, qc_ref, qb_ref, ka_ref, va_ref, kc_ref, vc_ref,
                     kb_ref, vb_ref, prev_ref, o_ref):
    del prev_ref
    n = CTX_LEN
    qa = _stack_gqa(qa_ref, n)
    oa = _softmax_pv([(_qk(qa, ka_ref[...]), va_ref[...])], n, sink_ref=sink_ref)
    for j, blk in enumerate(_unstack_gqa(oa, n)):
        o_ref[:, OCOL_A + j * LANE:OCOL_A + (j + 1) * LANE] = blk.astype(BF16)
    qb = _stack_mha(qb_ref, n)
    ob = _softmax_pv([(_qk(qb, kb_ref[...]), vb_ref[...])], n)
    o_ref[:, OCOL_B:OCOL_C] = _unstack_mha(ob, n).astype(BF16)
    qc = _stack_gqa(qc_ref, n)
    oc = _softmax_pv([(_qk(qc, kc_ref[...]), vc_ref[...])], n)
    for j, blk in enumerate(_unstack_gqa(oc, n)):
        o_ref[:, OCOL_C + j * LANE:OCOL_C + (j + 1) * LANE] = blk.astype(BF16)


def _ctx_attention(qkv, sink, o_prev, nb):
    ntok = qkv.shape[0]
    ctx_blk0 = nb * SEQ // CTX_LEN
    spec = lambda w, cb: pl.BlockSpec((CTX_LEN, w), lambda b: (ctx_blk0 + b, cb))
    return pl.pallas_call(
        _ctx_attn_kernel,
        out_shape=jax.ShapeDtypeStruct((ntok, D_MODEL), BF16),
        grid=(nb,),
        in_specs=[pl.BlockSpec(memory_space=pltpu.SMEM),
                  spec(384, COL_QA // 384), spec(384, COL_QC // 384), spec(256, COL_QB // 256)]
                 + [spec(w, cb) for w, cb in _kv_cols()]
                 + [pl.BlockSpec(memory_space=pl.ANY)],
        out_specs=pl.BlockSpec((CTX_LEN, D_MODEL), lambda b: (ctx_blk0 + b, 0)),
        input_output_aliases={10: 0},
        compiler_params=_compiler_params(1),
        name="ctx_attention",
    )(sink, *([qkv] * 9), o_prev)


def _ffn_kernel(o_ref, x_ref, gaa_ref, shf_ref, scf_ref, gaf_ref, g_ref,
                wo_ref, wg_ref, wu_ref, wd_ref, out_ref):
    a = jnp.dot(o_ref[...], wo_ref[...], preferred_element_type=F32)
    x1 = x_ref[...] + gaa_ref[0] * a
    h = _norm_modulate(x1, g_ref[...], scf_ref[0], shf_ref[0]).astype(BF16)
    y = None
    for c in range(FFN_HIDDEN // FFN_CHUNK):
        cs = slice(c * FFN_CHUNK, (c + 1) * FFN_CHUNK)
        u = jnp.dot(h, wg_ref[:, cs], preferred_element_type=F32)
        v = jnp.dot(h, wu_ref[:, cs], preferred_element_type=F32)
        act = (u / (1.0 + jnp.exp(-u)) * v).astype(BF16)
        yc = jnp.dot(act, wd_ref[cs, :], preferred_element_type=F32)
        y = yc if y is None else y + yc
    out_ref[...] = x1 + gaf_ref[0] * y


def _out_ffn(o, xx, mod3, g_ffn, w_out, w_gate, w_up, w_down, n_tiles, n_lat_tiles):
    tile = pl.BlockSpec((TM, D_MODEL), lambda i: (i, 0))
    return pl.pallas_call(
        _ffn_kernel,
        out_shape=jax.ShapeDtypeStruct((n_tiles * TM, D_MODEL), F32),
        grid=(n_tiles,),
        in_specs=[tile, tile] + [_mod_spec(k, n_lat_tiles) for k in (2, 3, 4, 5)]
                 + [_resident_spec((1, D_MODEL)), _resident_spec((D_MODEL, D_MODEL)),
                    _resident_spec((D_MODEL, FFN_HIDDEN)), _resident_spec((D_MODEL, FFN_HIDDEN)),
                    _resident_spec((FFN_HIDDEN, D_MODEL))],
        out_specs=tile,
        compiler_params=_compiler_params(1),
        name="out_ffn",
    )(o, xx, mod3, mod3, mod3, mod3, g_ffn, w_out, w_gate, w_up, w_down)


def kernel(x, c, ctx, c_ctx, w_mod, b_mod, norm_attn, norm_ffn, w_in, qk_norm, sink, rpb,
           w_out, w_gate, w_up, w_down):
    depth = w_mod.shape[0]
    nb = x.shape[0]
    assert x.shape == (nb, SEQ, D_MODEL) and ctx.shape == (nb, CTX_LEN, D_MODEL) and nb <= CTX_ROW
    assert (nb * CTX_LEN) % TM == 0
    n_lat_tiles = nb * LAT_TILES
    n_tiles = n_lat_tiles + nb * CTX_LEN // TM

    c8 = jnp.zeros((MOD_ROWS, D_MODEL), F32).at[:nb].set(c).at[CTX_ROW].set(c_ctx)
    mod = _modulation(c8, w_mod, b_mod)

    w_in_p = _permute(w_in, _in_perm(), 2).astype(BF16)
    w_out_p = _permute(w_out, _out_perm(), 1).astype(BF16)
    w_gate_b, w_up_b, w_down_b = w_gate.astype(BF16), w_up.astype(BF16), w_down.astype(BF16)
    cos_t, sin_t = _rope_tables()
    head_id = np.arange(2 * LANE) // HEAD_DIM
    bd = jnp.asarray(head_id[:, None] == head_id[None, :], BF16)
    row_classes, tbl_idx = _nb_row_classes()
    bias_all = _bias_tables(rpb * LOG2E, row_classes)
    tbl_idx = jnp.asarray(tbl_idx)

    xx = jnp.concatenate([x.reshape(nb * SEQ, D_MODEL), ctx.reshape(nb * CTX_LEN, D_MODEL)], axis=0)
    for l in range(depth):
        last = l == depth - 1
        mod3 = mod[l].reshape(MOD_ROWS, 1, 6 * D_MODEL)
        qkv = _in_projection(xx, mod3, norm_attn[l][None], w_in_p[l], _gain_vector(qk_norm[l])[None],
                             bd, cos_t, sin_t, n_lat_tiles)
        o = _attention(qkv, sink[l], bias_all[l], tbl_idx, nb)
        if not last:
            o = _ctx_attention(qkv, sink[l], o, nb)
        xx = _out_ffn(o, xx, mod3, norm_ffn[l][None], w_out_p[l], w_gate_b[l], w_up_b[l], w_down_b[l],
                      n_lat_tiles if last else n_tiles, n_lat_tiles)
    return xx.reshape(nb, SEQ, D_MODEL)
```

```python
import numpy as np
import jax
import jax.numpy as jnp
from jax import lax
from jax.experimental import pallas as pl
from jax.experimental.pallas import tpu as pltpu

D_MODEL = 1024
SEQ = 4096
CTX_LEN = 256
GRID_W = 64
ROWS = SEQ // GRID_W
HEAD_DIM = 64
A_HEADS = 6
B_HEADS = 4
C_HEADS = 6
GQA_G = 3
WINDOW = 128
NA_KH = 8
NA_KW = 16
FFN_HIDDEN = 2816
IN_WIDTH = 2048
ROPE_THETA = 10000.0
EPS = 1e-6
NEG_INF = -1e30
LOG2E = 1.4426950408889634
MOD_ROWS = 8
CTX_ROW = 4

LANE = 128
TM = 512
LAT_TILES = SEQ // TM
FFN_CHUNK = FFN_HIDDEN // 2
TQ = 128
NQ = SEQ // TQ
NQ_CTX = CTX_LEN // TQ
Q_ROWS = TQ // GRID_W
NB_ROWS = Q_ROWS + NA_KH
NB_KEYS = NB_ROWS * GRID_W
A_SPAN = TQ + 2 * WINDOW
C_CHUNK = 1024
PIECE = 128
VMEM_LIMIT = 56 * 1024 * 1024

COL_QA, COL_QC, COL_KA, COL_KC = 0, 384, 768, 896
COL_QB, COL_KB = 1024, 1280
COL_VA, COL_VC, COL_VB = 1536, 1664, 1792
OCOL_A, OCOL_B, OCOL_C = 0, 384, 640

F32 = jnp.float32
BF16 = jnp.bfloat16
NT_DIMS = (((1,), (1,)), ((), ()))


def _gqa_cols(base):
    cols = np.empty(GQA_G * 2 * HEAD_DIM, np.int32)
    for j in range(GQA_G):
        for g in range(2):
            dst = j * LANE + g * HEAD_DIM
            cols[dst:dst + HEAD_DIM] = base + (g * GQA_G + j) * HEAD_DIM + np.arange(HEAD_DIM)
    return cols


def _in_perm():
    r = np.arange
    return np.concatenate([
        _gqa_cols(0),
        _gqa_cols(1408),
        384 + r(128),
        1792 + r(128),
        640 + r(256),
        896 + r(256),
        512 + r(128),
        1920 + r(128),
        1152 + r(256),
    ]).astype(np.int32)


def _out_perm():
    return np.concatenate([_gqa_cols(0), 384 + np.arange(256), _gqa_cols(640)]).astype(np.int32)


def _permute(w, perm, axis):
    cuts = [0] + [k for k in range(1, len(perm)) if perm[k] != perm[k - 1] + 1] + [len(perm)]
    parts = [lax.slice_in_dim(w, int(perm[a]), int(perm[b - 1]) + 1, axis=axis)
             for a, b in zip(cuts[:-1], cuts[1:])]
    return jnp.concatenate(parts, axis=axis)


def _gain_vector(qk):
    sc = HEAD_DIM ** -0.5 * LOG2E
    t = jnp.tile
    return jnp.concatenate([
        t(qk[0] * sc, 6), t(qk[4] * sc, 6), t(qk[1], 2), t(qk[5], 2),
        t(qk[2] * sc, 4), t(qk[3], 4), jnp.ones((512,), F32)])


def _rope_tables():
    t = jnp.arange(SEQ, dtype=jnp.int32)
    row = (t // GRID_W).astype(F32)
    col = (t % GRID_W).astype(F32)
    n_freq = HEAD_DIM // 4
    inv = ROPE_THETA ** (-jnp.arange(n_freq, dtype=F32) / n_freq)
    ang = jnp.concatenate([row[:, None] * inv[None, :], col[:, None] * inv[None, :]], axis=-1)
    cos, sin = jnp.cos(ang), jnp.sin(ang)
    cos = jnp.concatenate([cos, jnp.ones((TM, HEAD_DIM // 2), F32)], axis=0)
    sin = jnp.concatenate([sin, jnp.zeros((TM, HEAD_DIM // 2), F32)], axis=0)
    cos_t = jnp.tile(cos, (1, 2 * LANE // HEAD_DIM))
    sin_t = jnp.tile(jnp.concatenate([-sin, sin], axis=-1), (1, LANE // HEAD_DIM))
    return cos_t, sin_t


def _nb_row_classes():
    pats = []
    for i in range(NQ):
        ks = int(np.clip(Q_ROWS * i - NA_KH // 2, 0, ROWS - NB_ROWS))
        r = Q_ROWS * i + np.arange(Q_ROWS)[:, None]
        krow = ks + np.arange(NB_ROWS)[None, :]
        rs = np.clip(r - NA_KH // 2, 0, ROWS - NA_KH)
        valid = (krow >= rs) & (krow < rs + NA_KH)
        pats.append(np.where(valid, krow - r + (NA_KH - 1), -1))
    uniq, inv = np.unique(np.stack(pats).reshape(NQ, -1), axis=0, return_inverse=True)
    return uniq.reshape(-1, Q_ROWS, NB_ROWS), inv.reshape(-1).astype(np.int32)


def _bias_tables(rpb, row_classes):
    w = GRID_W
    pad = jnp.pad(rpb, ((0, 0), (0, 0), (0, 0), (w, w)), constant_values=NEG_INF)
    cols = jnp.stack([lax.slice_in_dim(pad, w + NA_KW - 1 - c, 2 * w + NA_KW - 1 - c, axis=3)
                      for c in range(w)], axis=3)
    c = np.arange(w)
    ws = np.clip(c - NA_KW // 2, 0, w - NA_KW)
    col_ok = (c[None, :] >= ws[:, None]) & (c[None, :] < ws[:, None] + NA_KW)
    cols = jnp.where(col_ok, cols, NEG_INF)
    neg = jnp.full(cols.shape[:2] + (w, w), NEG_INF, F32)
    tables = []
    for cls in row_classes:
        rows = [jnp.concatenate([cols[:, :, int(d)] if d >= 0 else neg for d in cls[a]], axis=-1)
                for a in range(Q_ROWS)]
        tables.append(jnp.concatenate(rows, axis=-2))
    return jnp.stack(tables, axis=1)


def _compiler_params(n_axes, flags=None):
    return pltpu.CompilerParams(dimension_semantics=("arbitrary",) * n_axes,
                                vmem_limit_bytes=VMEM_LIMIT, flags=flags)


def _mod_kernel(c_ref, w_ref, b_ref, o_ref):
    c = c_ref[...]
    a = (c / (1.0 + jnp.exp(-c))).astype(BF16)
    o_ref[0] = jnp.dot(a, w_ref[0].astype(BF16), preferred_element_type=F32) + b_ref[0]


def _modulation(c8, w_mod, b_mod):
    depth = w_mod.shape[0]
    tn = 2048
    return pl.pallas_call(
        _mod_kernel,
        out_shape=jax.ShapeDtypeStruct((depth, MOD_ROWS, 6 * D_MODEL), F32),
        grid=(depth, 6 * D_MODEL // tn),
        in_specs=[pl.BlockSpec((MOD_ROWS, D_MODEL), lambda l, n: (0, 0)),
                  pl.BlockSpec((1, D_MODEL, tn), lambda l, n: (l, 0, n)),
                  pl.BlockSpec((1, 1, tn), lambda l, n: (l, 0, n))],
        out_specs=pl.BlockSpec((1, MOD_ROWS, tn), lambda l, n: (l, 0, n)),
        compiler_params=_compiler_params(2),
        name="modulation",
    )(c8, w_mod, b_mod.reshape(depth, 1, 6 * D_MODEL))


def _mod_spec(chunk, n_lat_tiles):
    return pl.BlockSpec(
        (1, 1, D_MODEL),
        lambda i: (jnp.where(i < n_lat_tiles, i // LAT_TILES, CTX_ROW), 0, chunk))


def _resident_spec(shape):
    return pl.BlockSpec(shape, lambda i: (0,) * len(shape), pipeline_mode=pl.Buffered(1))


def _norm_modulate(x, g, scale, shift):
    ms = jnp.mean(x * x, axis=-1, keepdims=True)
    return x * lax.rsqrt(ms + EPS) * (g * (1.0 + scale)) + shift


def _inproj_kernel(x_ref, sh_ref, sc_ref, g_ref, w_ref, gv_ref, bd_ref, cos_ref, sin_ref, o_ref):
    h = _norm_modulate(x_ref[...], g_ref[...], sc_ref[0], sh_ref[0]).astype(BF16)
    lane = lax.broadcasted_iota(jnp.int32, (TM, LANE), 1)
    first_half = (lane % HEAD_DIM) < (HEAD_DIM // 2)
    cos = cos_ref[...]
    sin = sin_ref[...]
    cw = 2 * LANE
    proj = jnp.dot(h, w_ref[...], preferred_element_type=F32)
    for c in range(IN_WIDTH // cw):
        cs = slice(c * cw, (c + 1) * cw)
        p = proj[:, cs]
        if c * cw < COL_VA:
            ss = jnp.dot((p * p).astype(BF16), bd_ref[...], preferred_element_type=F32)
            p = p * lax.rsqrt(ss * (1.0 / HEAD_DIM) + EPS) * gv_ref[:, cs]
        if c * cw < COL_QB:
            halves = []
            for s in range(2):
                xh = p[:, s * LANE:(s + 1) * LANE]
                partner = jnp.where(first_half,
                                    pltpu.roll(xh, LANE - HEAD_DIM // 2, 1),
                                    pltpu.roll(xh, HEAD_DIM // 2, 1))
                halves.append(xh * cos + partner * sin)
            p = jnp.concatenate(halves, axis=1)
        o_ref[:, cs] = p.astype(BF16)


def _in_projection(xx, mod3, g_attn, w_in, gvec, bd, cos_t, sin_t, n_lat_tiles):
    ntok = xx.shape[0]
    rope_spec = pl.BlockSpec(
        (TM, LANE), lambda i: (jnp.where(i < n_lat_tiles, i % LAT_TILES, LAT_TILES), 0))
    return pl.pallas_call(
        _inproj_kernel,
        out_shape=jax.ShapeDtypeStruct((ntok, IN_WIDTH), BF16),
        grid=(ntok // TM,),
        in_specs=[pl.BlockSpec((TM, D_MODEL), lambda i: (i, 0)),
                  _mod_spec(0, n_lat_tiles), _mod_spec(1, n_lat_tiles),
                  _resident_spec((1, D_MODEL)), _resident_spec((D_MODEL, IN_WIDTH)),
                  _resident_spec((1, IN_WIDTH)), _resident_spec((2 * LANE, 2 * LANE)),
                  rope_spec, rope_spec],
        out_specs=pl.BlockSpec((TM, IN_WIDTH), lambda i: (i, 0)),
        compiler_params=_compiler_params(1),
        name="in_projection",
    )(xx, mod3, mod3, g_attn, w_in, gvec, bd, cos_t, sin_t)


def _stack_gqa(q_ref, nq):
    low = lax.broadcasted_iota(jnp.int32, (nq, LANE), 1) < HEAD_DIM
    blocks = []
    for g in range(2):
        for j in range(GQA_G):
            qj = q_ref[:, j * LANE:(j + 1) * LANE]
            blocks.append(jnp.where(low if g == 0 else jnp.logical_not(low), qj, jnp.zeros_like(qj)))
    return jnp.concatenate(blocks, axis=0)


def _unstack_gqa(o, nq):
    low = lax.broadcasted_iota(jnp.int32, (nq, LANE), 1) < HEAD_DIM
    return [jnp.where(low, o[j * nq:(j + 1) * nq], o[(GQA_G + j) * nq:(GQA_G + j + 1) * nq])
            for j in range(GQA_G)]


def _stack_mha(q_ref, nq):
    q = q_ref[...]
    head = lax.broadcasted_iota(jnp.int32, (nq, B_HEADS * HEAD_DIM), 1) // HEAD_DIM
    return jnp.concatenate([jnp.where(head == h, q, jnp.zeros_like(q)) for h in range(B_HEADS)], axis=0)


def _unstack_mha(o, nq):
    head = lax.broadcasted_iota(jnp.int32, (nq, B_HEADS * HEAD_DIM), 1) // HEAD_DIM
    out = jnp.where(head == 0, o[0:nq], 0.0)
    for h in range(1, B_HEADS):
        out = jnp.where(head == h, o[h * nq:(h + 1) * nq], out)
    return out


def _qk(q, k):
    return lax.dot_general(q, k, NT_DIMS, preferred_element_type=F32)


def _softmax_pv(parts, nq, sink_ref=None, adjust=None):
    rows = parts[0][0].shape[0]
    width = parts[0][1].shape[1]
    mxu_sums = width == LANE
    probs = [[] for _ in parts]
    dens = []
    for r0 in range(0, rows, PIECE):
        sl = [s[r0:r0 + PIECE] for s, _ in parts]
        if adjust is not None:
            sl[0] = adjust(sl[0], r0 // nq, r0 % nq)
        m = None
        for sp in sl:
            sm = jnp.max(sp, axis=-1, keepdims=True)
            m = sm if m is None else jnp.maximum(m, sm)
        den = None
        if sink_ref is not None:
            sink = sink_ref[r0 // nq] * LOG2E
            m = jnp.maximum(m, sink)
            den = jnp.exp2(sink - m)
        for k, sp in enumerate(sl):
            p = jnp.exp2(sp - m)
            if not mxu_sums:
                ps = jnp.sum(p, axis=-1, keepdims=True)
                den = ps if den is None else den + ps
            probs[k].append(p.astype(BF16))
        if den is not None:
            dens.append(den)
    acc = None
    for k, (_, v) in enumerate(parts):
        if mxu_sums:
            v = jnp.concatenate([v, jnp.ones(v.shape, BF16)], axis=1)
        pv = jnp.dot(jnp.concatenate(probs[k], axis=0), v, preferred_element_type=F32)
        acc = pv if acc is None else acc + pv
    if not mxu_sums:
        return acc * (1.0 / jnp.concatenate(dens, axis=0))
    den = acc[:, width:]
    if dens:
        den = den + jnp.concatenate(dens, axis=0)
    return acc[:, :width] * (1.0 / den)


def _global_attention(q, k_ref, v_ref, kx_ref, vx_ref):
    chunks = [(k_ref, v_ref, slice(k0, k0 + C_CHUNK)) for k0 in range(0, SEQ, C_CHUNK)]
    chunks.append((kx_ref, vx_ref, slice(0, CTX_LEN)))
    m = acc = None
    for kr, vr, ks in chunks:
        s = _qk(q, kr[ks, :])
        v_ext = jnp.concatenate([vr[ks, :], jnp.ones((ks.stop - ks.start, LANE), BF16)], axis=1)
        sm = jnp.max(s, axis=-1, keepdims=True)
        m_new = sm if m is None else jnp.maximum(m, sm)
        pv = jnp.dot(jnp.exp2(s - m_new).astype(BF16), v_ext, preferred_element_type=F32)
        acc = pv if m is None else acc * jnp.exp2(m - m_new) + pv
        m = m_new
    return acc[:, :LANE] * (1.0 / acc[:, LANE:])


def _attn_kernel(tbl_ref, sink_ref, qa_ref, qc_ref, qb_ref,
                 ka_ref, va_ref, kc_ref, vc_ref, kb_ref, vb_ref,
                 kax_ref, vax_ref, kcx_ref, vcx_ref, kbx_ref, vbx_ref, bias_ref, o_ref):
    del tbl_ref
    i = pl.program_id(1)

    def store(oa, ob, oc):
        for j, blk in enumerate(_unstack_gqa(oa, TQ)):
            o_ref[:, OCOL_A + j * LANE:OCOL_A + (j + 1) * LANE] = blk.astype(BF16)
        o_ref[:, OCOL_B:OCOL_C] = _unstack_mha(ob, TQ).astype(BF16)
        for j, blk in enumerate(_unstack_gqa(oc, TQ)):
            o_ref[:, OCOL_C + j * LANE:OCOL_C + (j + 1) * LANE] = blk.astype(BF16)

    @pl.when(i < NQ)
    def _latent_queries():
        qa = _stack_gqa(qa_ref, TQ)
        start = pl.multiple_of(jnp.clip(i * TQ - WINDOW, 0, SEQ - A_SPAN), LANE)
        na = A_SPAN + CTX_LEN
        col = lax.broadcasted_iota(jnp.int32, (TQ, na), 1)
        qpos = i * TQ + lax.broadcasted_iota(jnp.int32, (TQ, na), 0)
        valid = (jnp.abs(qpos - (start + col)) <= WINDOW) | (col >= A_SPAN)
        k_all = jnp.concatenate([ka_ref[pl.ds(start, A_SPAN), :], kax_ref[...]], axis=0)
        v_all = jnp.concatenate([va_ref[pl.ds(start, A_SPAN), :], vax_ref[...]], axis=0)
        oa = _softmax_pv([(_qk(qa, k_all), v_all)], TQ, sink_ref=sink_ref,
                         adjust=lambda sp, n, q0: jnp.where(valid[q0:q0 + PIECE], sp, NEG_INF))

        qb = _stack_mha(qb_ref, TQ)
        krow0 = jnp.clip(Q_ROWS * i - NA_KH // 2, 0, ROWS - NB_ROWS)
        kstart = pl.multiple_of(krow0 * GRID_W, GRID_W)
        k_all = jnp.concatenate([kb_ref[pl.ds(kstart, NB_KEYS), :], kbx_ref[...]], axis=0)
        v_all = jnp.concatenate([vb_ref[pl.ds(kstart, NB_KEYS), :], vbx_ref[...]], axis=0)
        ob = _softmax_pv(
            [(_qk(qb, k_all), v_all)], TQ,
            adjust=lambda sp, n, q0: jnp.concatenate(
                [sp[:, :NB_KEYS] + bias_ref[0, n, q0:q0 + PIECE, :], sp[:, NB_KEYS:]], axis=1))

        oc = _global_attention(_stack_gqa(qc_ref, TQ), kc_ref, vc_ref, kcx_ref, vcx_ref)
        store(oa, ob, oc)

    @pl.when(i >= NQ)
    def _context_queries():
        qa = _stack_gqa(qa_ref, TQ)
        oa = _softmax_pv([(_qk(qa, kax_ref[...]), vax_ref[...])], TQ, sink_ref=sink_ref)
        qb = _stack_mha(qb_ref, TQ)
        ob = _softmax_pv([(_qk(qb, kbx_ref[...]), vbx_ref[...])], TQ)
        qc = _stack_gqa(qc_ref, TQ)
        oc = _softmax_pv([(_qk(qc, kcx_ref[...]), vcx_ref[...])], TQ)
        store(oa, ob, oc)


def _kv_cols():
    return [(128, COL_KA // 128), (128, COL_VA // 128), (128, COL_KC // 128), (128, COL_VC // 128),
            (256, COL_KB // 256), (256, COL_VB // 256)]


def _attention(qkv, sink, bias_tbl, tbl_idx, nb, with_ctx):
    ctx_blk0 = nb * SEQ // CTX_LEN
    ctx_q0 = nb * NQ
    n_blocks = NQ + (NQ_CTX if with_ctx else 0)
    qrow = lambda b, i: jnp.where(i < NQ, b * NQ + i, ctx_q0 + b * NQ_CTX + i - NQ)
    qspec = lambda w, cb: pl.BlockSpec((TQ, w), lambda b, i, t: (qrow(b, i), cb))
    lat = lambda w, cb: pl.BlockSpec((SEQ, w), lambda b, i, t: (b, cb))
    cxt = lambda w, cb: pl.BlockSpec((CTX_LEN, w), lambda b, i, t: (ctx_blk0 + b, cb))
    grid_spec = pltpu.PrefetchScalarGridSpec(
        num_scalar_prefetch=1,
        grid=(nb, n_blocks),
        in_specs=[pl.BlockSpec(memory_space=pltpu.SMEM),
                  qspec(384, COL_QA // 384), qspec(384, COL_QC // 384), qspec(256, COL_QB // 256)]
                 + [lat(w, cb) for w, cb in _kv_cols()] + [cxt(w, cb) for w, cb in _kv_cols()]
                 + [pl.BlockSpec((1, B_HEADS, TQ, NB_KEYS), lambda b, i, t: (t[i], 0, 0, 0))],
        out_specs=pl.BlockSpec((TQ, D_MODEL), lambda b, i, t: (qrow(b, i), 0)),
    )
    return pl.pallas_call(
        _attn_kernel,
        out_shape=jax.ShapeDtypeStruct((nb * n_blocks * TQ, D_MODEL), BF16),
        grid_spec=grid_spec,
        compiler_params=_compiler_params(2),
        name="attention",
    )(tbl_idx, sink, *([qkv] * 15), bias_tbl)


def _ffn_kernel(o_ref, x_ref, gaa_ref, shf_ref, scf_ref, gaf_ref, g_ref,
                wo_ref, wg_ref, wu_ref, wd_ref, out_ref):
    a = jnp.dot(o_ref[...], wo_ref[...], preferred_element_type=F32)
    x1 = x_ref[...] + gaa_ref[0] * a
    h = _norm_modulate(x1, g_ref[...], scf_ref[0], shf_ref[0]).astype(BF16)
    y = None
    for c in range(FFN_HIDDEN // FFN_CHUNK):
        cs = slice(c * FFN_CHUNK, (c + 1) * FFN_CHUNK)
        u = jnp.dot(h, wg_ref[:, cs], preferred_element_type=F32)
        v = jnp.dot(h, wu_ref[:, cs], preferred_element_type=F32)
        act = (u / (1.0 + jnp.exp(-u)) * v).astype(BF16)
        yc = jnp.dot(act, wd_ref[cs, :], preferred_element_type=F32)
        y = yc if y is None else y + yc
    out_ref[...] = x1 + gaf_ref[0] * y


def _out_ffn(o, xx, mod3, g_ffn, w_out, w_gate, w_up, w_down, n_tiles, n_lat_tiles):
    tile = pl.BlockSpec((TM, D_MODEL), lambda i: (i, 0))
    return pl.pallas_call(
        _ffn_kernel,
        out_shape=jax.ShapeDtypeStruct((n_tiles * TM, D_MODEL), F32),
        grid=(n_tiles,),
        in_specs=[tile, tile] + [_mod_spec(k, n_lat_tiles) for k in (2, 3, 4, 5)]
                 + [_resident_spec((1, D_MODEL)), _resident_spec((D_MODEL, D_MODEL)),
                    _resident_spec((D_MODEL, FFN_HIDDEN)), _resident_spec((D_MODEL, FFN_HIDDEN)),
                    _resident_spec((FFN_HIDDEN, D_MODEL))],
        out_specs=tile,
        compiler_params=_compiler_params(1),
        name="out_ffn",
    )(o, xx, mod3, mod3, mod3, mod3, g_ffn, w_out, w_gate, w_up, w_down)


def kernel(x, c, ctx, c_ctx, w_mod, b_mod, norm_attn, norm_ffn, w_in, qk_norm, sink, rpb,
           w_out, w_gate, w_up, w_down):
    depth = w_mod.shape[0]
    nb = x.shape[0]
    assert x.shape == (nb, SEQ, D_MODEL) and ctx.shape == (nb, CTX_LEN, D_MODEL) and nb <= CTX_ROW
    assert (nb * CTX_LEN) % TM == 0
    n_lat_tiles = nb * LAT_TILES
    n_tiles = n_lat_tiles + nb * CTX_LEN // TM

    c8 = jnp.zeros((MOD_ROWS, D_MODEL), F32).at[:nb].set(c).at[CTX_ROW].set(c_ctx)
    mod = _modulation(c8, w_mod, b_mod)

    w_in_p = _permute(w_in, _in_perm(), 2).astype(BF16)
    w_out_p = _permute(w_out, _out_perm(), 1).astype(BF16)
    w_gate_b, w_up_b, w_down_b = w_gate.astype(BF16), w_up.astype(BF16), w_down.astype(BF16)
    cos_t, sin_t = _rope_tables()
    head_id = np.arange(2 * LANE) // HEAD_DIM
    bd = jnp.asarray(head_id[:, None] == head_id[None, :], BF16)
    row_classes, tbl_idx = _nb_row_classes()
    bias_all = _bias_tables(rpb * LOG2E, row_classes)
    tbl_idx = jnp.asarray(np.concatenate([tbl_idx, np.full(NQ_CTX, tbl_idx[-1], np.int32)]))

    xx = jnp.concatenate([x.reshape(nb * SEQ, D_MODEL), ctx.reshape(nb * CTX_LEN, D_MODEL)], axis=0)
    for l in range(depth):
        last = l == depth - 1
        mod3 = mod[l].reshape(MOD_ROWS, 1, 6 * D_MODEL)
        qkv = _in_projection(xx, mod3, norm_attn[l][None], w_in_p[l], _gain_vector(qk_norm[l])[None],
                             bd, cos_t, sin_t, n_lat_tiles)
        o = _attention(qkv, sink[l], bias_all[l], tbl_idx, nb, not last)
        xx = _out_ffn(o, xx, mod3, norm_ffn[l][None], w_out_p[l], w_gate_b[l], w_up_b[l], w_down_b[l],
                      n_lat_tiles if last else n_tiles, n_lat_tiles)
    return xx.reshape(nb, SEQ, D_MODEL)
```

```python
import numpy as np
import jax
import jax.numpy as jnp
from jax import lax
from jax.experimental import pallas as pl
from jax.experimental.pallas import tpu as pltpu

D_MODEL = 1024
SEQ = 4096
CTX_LEN = 256
GRID_W = 64
ROWS = SEQ // GRID_W
HEAD_DIM = 64
A_HEADS = 6
B_HEADS = 4
C_HEADS = 6
GQA_G = 3
WINDOW = 128
NA_KH = 8
NA_KW = 16
FFN_HIDDEN = 2816
IN_WIDTH = 2048
ROPE_THETA = 10000.0
EPS = 1e-6
NEG_INF = -1e30
LOG2E = 1.4426950408889634
MOD_ROWS = 8
CTX_ROW = 4

LANE = 128
TM = 512
LAT_TILES = SEQ // TM
MXU_DIM = 256
FFN_SPLITS = (0, (FFN_HIDDEN // MXU_DIM + 1) // 2 * MXU_DIM, FFN_HIDDEN)
TQ = 128
NQ = SEQ // TQ
NQ_CTX = CTX_LEN // TQ
BPS = 2
Q_ROWS = TQ // GRID_W
NB_ROWS = Q_ROWS + NA_KH
NB_KEYS = NB_ROWS * GRID_W
A_SPAN = TQ + 2 * WINDOW
C_CHUNK = 1024
PIECE = 128
VMEM_LIMIT = 56 * 1024 * 1024

COL_QA, COL_QC, COL_KA, COL_KC = 0, 384, 768, 896
COL_QB, COL_KB = 1024, 1280
COL_VA, COL_VC, COL_VB = 1536, 1664, 1792
OCOL_A, OCOL_B, OCOL_C = 0, 384, 640

F32 = jnp.float32
BF16 = jnp.bfloat16
NT_DIMS = (((1,), (1,)), ((), ()))


def _gqa_cols(base):
    cols = np.empty(GQA_G * 2 * HEAD_DIM, np.int32)
    for j in range(GQA_G):
        for g in range(2):
            dst = j * LANE + g * HEAD_DIM
            cols[dst:dst + HEAD_DIM] = base + (g * GQA_G + j) * HEAD_DIM + np.arange(HEAD_DIM)
    return cols


def _in_perm():
    r = np.arange
    return np.concatenate([
        _gqa_cols(0),
        _gqa_cols(1408),
        384 + r(128),
        1792 + r(128),
        640 + r(256),
        896 + r(256),
        512 + r(128),
        1920 + r(128),
        1152 + r(256),
    ]).astype(np.int32)


def _out_perm():
    return np.concatenate([_gqa_cols(0), 384 + np.arange(256), _gqa_cols(640)]).astype(np.int32)


def _permute(w, perm, axis):
    cuts = [0] + [k for k in range(1, len(perm)) if perm[k] != perm[k - 1] + 1] + [len(perm)]
    parts = [lax.slice_in_dim(w, int(perm[a]), int(perm[b - 1]) + 1, axis=axis)
             for a, b in zip(cuts[:-1], cuts[1:])]
    return jnp.concatenate(parts, axis=axis)


def _gain_vector(qk):
    sc = HEAD_DIM ** -0.5 * LOG2E
    t = jnp.tile
    return jnp.concatenate([
        t(qk[0] * sc, 6), t(qk[4] * sc, 6), t(qk[1], 2), t(qk[5], 2),
        t(qk[2] * sc, 4), t(qk[3], 4), jnp.ones((512,), F32)])


def _rope_tables():
    t = jnp.arange(SEQ, dtype=jnp.int32)
    row = (t // GRID_W).astype(F32)
    col = (t % GRID_W).astype(F32)
    n_freq = HEAD_DIM // 4
    inv = ROPE_THETA ** (-jnp.arange(n_freq, dtype=F32) / n_freq)
    ang = jnp.concatenate([row[:, None] * inv[None, :], col[:, None] * inv[None, :]], axis=-1)
    cos, sin = jnp.cos(ang), jnp.sin(ang)
    cos = jnp.concatenate([cos, jnp.ones((TM, HEAD_DIM // 2), F32)], axis=0)
    sin = jnp.concatenate([sin, jnp.zeros((TM, HEAD_DIM // 2), F32)], axis=0)
    cos_t = jnp.tile(cos, (1, 2 * LANE // HEAD_DIM))
    sin_t = jnp.tile(jnp.concatenate([-sin, sin], axis=-1), (1, LANE // HEAD_DIM))
    return cos_t, sin_t


def _nb_row_classes():
    pats = []
    for i in range(NQ):
        ks = int(np.clip(Q_ROWS * i - NA_KH // 2, 0, ROWS - NB_ROWS))
        r = Q_ROWS * i + np.arange(Q_ROWS)[:, None]
        krow = ks + np.arange(NB_ROWS)[None, :]
        rs = np.clip(r - NA_KH // 2, 0, ROWS - NA_KH)
        valid = (krow >= rs) & (krow < rs + NA_KH)
        pats.append(np.where(valid, krow - r + (NA_KH - 1), -1))
    uniq, inv = np.unique(np.stack(pats).reshape(NQ, -1), axis=0, return_inverse=True)
    return uniq.reshape(-1, Q_ROWS, NB_ROWS), inv.reshape(-1).astype(np.int32)


def _bias_tables(rpb, row_classes):
    w = GRID_W
    pad = jnp.pad(rpb, ((0, 0), (0, 0), (0, 0), (w, w)), constant_values=NEG_INF)
    cols = jnp.stack([lax.slice_in_dim(pad, w + NA_KW - 1 - c, 2 * w + NA_KW - 1 - c, axis=3)
                      for c in range(w)], axis=3)
    c = np.arange(w)
    ws = np.clip(c - NA_KW // 2, 0, w - NA_KW)
    col_ok = (c[None, :] >= ws[:, None]) & (c[None, :] < ws[:, None] + NA_KW)
    cols = jnp.where(col_ok, cols, NEG_INF)
    neg = jnp.full(cols.shape[:2] + (w, w), NEG_INF, F32)
    tables = []
    for cls in row_classes:
        rows = [jnp.concatenate([cols[:, :, int(d)] if d >= 0 else neg for d in cls[a]], axis=-1)
                for a in range(Q_ROWS)]
        tables.append(jnp.concatenate(rows, axis=-2))
    return jnp.stack(tables, axis=1)


def _compiler_params(n_axes, flags=None):
    return pltpu.CompilerParams(dimension_semantics=("arbitrary",) * n_axes,
                                vmem_limit_bytes=VMEM_LIMIT, flags=flags)


def _mod_kernel(c_ref, w_ref, b_ref, o_ref):
    c = c_ref[...]
    a = (c / (1.0 + jnp.exp(-c))).astype(BF16)
    o_ref[0] = jnp.dot(a, w_ref[0].astype(BF16), preferred_element_type=F32) + b_ref[0]


def _modulation(c8, w_mod, b_mod):
    depth = w_mod.shape[0]
    tn = 2048
    return pl.pallas_call(
        _mod_kernel,
        out_shape=jax.ShapeDtypeStruct((depth, MOD_ROWS, 6 * D_MODEL), F32),
        grid=(depth, 6 * D_MODEL // tn),
        in_specs=[pl.BlockSpec((MOD_ROWS, D_MODEL), lambda l, n: (0, 0)),
                  pl.BlockSpec((1, D_MODEL, tn), lambda l, n: (l, 0, n)),
                  pl.BlockSpec((1, 1, tn), lambda l, n: (l, 0, n))],
        out_specs=pl.BlockSpec((1, MOD_ROWS, tn), lambda l, n: (l, 0, n)),
        compiler_params=_compiler_params(2),
        name="modulation",
    )(c8, w_mod, b_mod.reshape(depth, 1, 6 * D_MODEL))


def _mod_spec(chunk, n_lat_tiles):
    return pl.BlockSpec(
        (1, 1, D_MODEL),
        lambda i: (jnp.where(i < n_lat_tiles, i // LAT_TILES, CTX_ROW), 0, chunk))


def _resident_spec(shape):
    return pl.BlockSpec(shape, lambda i: (0,) * len(shape), pipeline_mode=pl.Buffered(1))


def _norm_modulate(x, g, scale, shift):
    ms = jnp.mean(x * x, axis=-1, keepdims=True)
    return x * lax.rsqrt(ms + EPS) * (g * (1.0 + scale)) + shift


def _inproj_kernel(x_ref, sh_ref, sc_ref, g_ref, w_ref, gv_ref, bd_ref, cos_ref, sin_ref, o_ref):
    h = _norm_modulate(x_ref[...], g_ref[...], sc_ref[0], sh_ref[0]).astype(BF16)
    lane = lax.broadcasted_iota(jnp.int32, (TM, LANE), 1)
    first_half = (lane % HEAD_DIM) < (HEAD_DIM // 2)
    cos = cos_ref[...]
    sin = sin_ref[...]
    cw = 2 * LANE
    proj = jnp.dot(h, w_ref[...], preferred_element_type=F32)
    for c in range(IN_WIDTH // cw):
        cs = slice(c * cw, (c + 1) * cw)
        p = proj[:, cs]
        if c * cw < COL_VA:
            ms = jnp.dot((p * p).astype(BF16), bd_ref[...], preferred_element_type=F32)
            p = p * lax.rsqrt(ms + EPS) * gv_ref[:, cs]
        if c * cw < COL_QB:
            halves = []
            for s in range(2):
                xh = p[:, s * LANE:(s + 1) * LANE]
                partner = jnp.where(first_half,
                                    pltpu.roll(xh, LANE - HEAD_DIM // 2, 1),
                                    pltpu.roll(xh, HEAD_DIM // 2, 1))
                halves.append(xh * cos + partner * sin)
            p = jnp.concatenate(halves, axis=1)
        o_ref[:, cs] = p.astype(BF16)


def _in_projection(xx, mod3, g_attn, w_in, gvec, bd, cos_t, sin_t, n_lat_tiles):
    ntok = xx.shape[0]
    rope_spec = pl.BlockSpec(
        (TM, LANE), lambda i: (jnp.where(i < n_lat_tiles, i % LAT_TILES, LAT_TILES), 0))
    return pl.pallas_call(
        _inproj_kernel,
        out_shape=jax.ShapeDtypeStruct((ntok, IN_WIDTH), BF16),
        grid=(ntok // TM,),
        in_specs=[pl.BlockSpec((TM, D_MODEL), lambda i: (i, 0)),
                  _mod_spec(0, n_lat_tiles), _mod_spec(1, n_lat_tiles),
                  _resident_spec((1, D_MODEL)), _resident_spec((D_MODEL, IN_WIDTH)),
                  _resident_spec((1, IN_WIDTH)), _resident_spec((2 * LANE, 2 * LANE)),
                  rope_spec, rope_spec],
        out_specs=pl.BlockSpec((TM, IN_WIDTH), lambda i: (i, 0)),
        compiler_params=_compiler_params(1),
        name="in_projection",
    )(xx, mod3, mod3, g_attn, w_in, gvec, bd, cos_t, sin_t)


def _stack_gqa(q_ref, nq):
    low = lax.broadcasted_iota(jnp.int32, (nq, LANE), 1) < HEAD_DIM
    blocks = []
    for g in range(2):
        for j in range(GQA_G):
            qj = q_ref[:, j * LANE:(j + 1) * LANE]
            blocks.append(jnp.where(low if g == 0 else jnp.logical_not(low), qj, jnp.zeros_like(qj)))
    return jnp.concatenate(blocks, axis=0)


def _unstack_gqa(o, nq):
    low = lax.broadcasted_iota(jnp.int32, (nq, LANE), 1) < HEAD_DIM
    return [jnp.where(low, o[j * nq:(j + 1) * nq], o[(GQA_G + j) * nq:(GQA_G + j + 1) * nq])
            for j in range(GQA_G)]


def _stack_mha(q_ref, nq):
    q = q_ref[...]
    head = lax.broadcasted_iota(jnp.int32, (nq, B_HEADS * HEAD_DIM), 1) // HEAD_DIM
    return jnp.concatenate([jnp.where(head == h, q, jnp.zeros_like(q)) for h in range(B_HEADS)], axis=0)


def _unstack_mha(o, nq):
    head = lax.broadcasted_iota(jnp.int32, (nq, B_HEADS * HEAD_DIM), 1) // HEAD_DIM
    out = jnp.where(head == 0, o[0:nq], 0.0)
    for h in range(1, B_HEADS):
        out = jnp.where(head == h, o[h * nq:(h + 1) * nq], out)
    return out


def _qk(q, k):
    return lax.dot_general(q, k, NT_DIMS, preferred_element_type=F32)


def _softmax_pv(parts, nq, sink_ref=None, adjust=None):
    rows = parts[0][0].shape[0]
    width = parts[0][1].shape[1]
    mxu_sums = width == LANE
    probs = [[] for _ in parts]
    dens = []
    for r0 in range(0, rows, PIECE):
        sl = [s[r0:r0 + PIECE] for s, _ in parts]
        if adjust is not None:
            sl[0] = adjust(sl[0], r0 // nq, r0 % nq)
        m = None
        for sp in sl:
            sm = jnp.max(sp, axis=-1, keepdims=True)
            m = sm if m is None else jnp.maximum(m, sm)
        den = None
        if sink_ref is not None:
            sink = sink_ref[r0 // nq] * LOG2E
            m = jnp.maximum(m, sink)
            den = jnp.exp2(sink - m)
        for k, sp in enumerate(sl):
            p = jnp.exp2(sp - m)
            if not mxu_sums:
                ps = jnp.sum(p, axis=-1, keepdims=True)
                den = ps if den is None else den + ps
            probs[k].append(p.astype(BF16))
        if den is not None:
            dens.append(den)
    acc = None
    for k, (_, v) in enumerate(parts):
        if mxu_sums:
            v = jnp.concatenate([v, jnp.ones(v.shape, BF16)], axis=1)
        pv = jnp.dot(jnp.concatenate(probs[k], axis=0), v, preferred_element_type=F32)
        acc = pv if acc is None else acc + pv
    if not mxu_sums:
        return acc * (1.0 / jnp.concatenate(dens, axis=0))
    den = acc[:, width:]
    if dens:
        den = den + jnp.concatenate(dens, axis=0)
    return acc[:, :width] * (1.0 / den)


def _global_attention(q, k_ref, v_ref, kx_ref, vx_ref):
    chunks = [(k_ref, v_ref, slice(k0, k0 + C_CHUNK)) for k0 in range(0, SEQ, C_CHUNK)]
    chunks.append((kx_ref, vx_ref, slice(0, CTX_LEN)))
    m = acc = None
    for kr, vr, ks in chunks:
        s = _qk(q, kr[ks, :])
        v_ext = jnp.concatenate([vr[ks, :], jnp.ones((ks.stop - ks.start, LANE), BF16)], axis=1)
        sm = jnp.max(s, axis=-1, keepdims=True)
        m_new = sm if m is None else jnp.maximum(m, sm)
        pv = jnp.dot(jnp.exp2(s - m_new).astype(BF16), v_ext, preferred_element_type=F32)
        acc = pv if m is None else acc * jnp.exp2(m - m_new) + pv
        m = m_new
    return acc[:, :LANE] * (1.0 / acc[:, LANE:])


def _attn_kernel(tbl_ref, sink_ref, qa_all, qc_all, qb_all,
                 ka_ref, va_ref, kc_ref, vc_ref, kb_ref, vb_ref,
                 kax_ref, vax_ref, kcx_ref, vcx_ref, kbx_ref, vbx_ref, *rest):
    del tbl_ref
    bias_refs, o_all = rest[:BPS], rest[BPS]
    step = pl.program_id(1)

    def store(o_ref, oa, ob, oc):
        for j, blk in enumerate(_unstack_gqa(oa, TQ)):
            o_ref[:, OCOL_A + j * LANE:OCOL_A + (j + 1) * LANE] = blk.astype(BF16)
        o_ref[:, OCOL_B:OCOL_C] = _unstack_mha(ob, TQ).astype(BF16)
        for j, blk in enumerate(_unstack_gqa(oc, TQ)):
            o_ref[:, OCOL_C + j * LANE:OCOL_C + (j + 1) * LANE] = blk.astype(BF16)

    def latent_block(i, qa_ref, qc_ref, qb_ref, bias_ref, o_ref):
        qa = _stack_gqa(qa_ref, TQ)
        start = pl.multiple_of(jnp.clip(i * TQ - WINDOW, 0, SEQ - A_SPAN), LANE)
        na = A_SPAN + CTX_LEN
        col = lax.broadcasted_iota(jnp.int32, (TQ, na), 1)
        qpos = i * TQ + lax.broadcasted_iota(jnp.int32, (TQ, na), 0)
        valid = (jnp.abs(qpos - (start + col)) <= WINDOW) | (col >= A_SPAN)
        k_all = jnp.concatenate([ka_ref[pl.ds(start, A_SPAN), :], kax_ref[...]], axis=0)
        v_all = jnp.concatenate([va_ref[pl.ds(start, A_SPAN), :], vax_ref[...]], axis=0)
        oa = _softmax_pv([(_qk(qa, k_all), v_all)], TQ, sink_ref=sink_ref,
                         adjust=lambda sp, n, q0: jnp.where(valid[q0:q0 + PIECE], sp, NEG_INF))

        qb = _stack_mha(qb_ref, TQ)
        krow0 = jnp.clip(Q_ROWS * i - NA_KH // 2, 0, ROWS - NB_ROWS)
        kstart = pl.multiple_of(krow0 * GRID_W, GRID_W)
        k_all = jnp.concatenate([kb_ref[pl.ds(kstart, NB_KEYS), :], kbx_ref[...]], axis=0)
        v_all = jnp.concatenate([vb_ref[pl.ds(kstart, NB_KEYS), :], vbx_ref[...]], axis=0)
        ob = _softmax_pv(
            [(_qk(qb, k_all), v_all)], TQ,
            adjust=lambda sp, n, q0: jnp.concatenate(
                [sp[:, :NB_KEYS] + bias_ref[0, n, q0:q0 + PIECE, :], sp[:, NB_KEYS:]], axis=1))

        oc = _global_attention(_stack_gqa(qc_ref, TQ), kc_ref, vc_ref, kcx_ref, vcx_ref)
        store(o_ref, oa, ob, oc)

    def context_block(qa_ref, qc_ref, qb_ref, o_ref):
        qa = _stack_gqa(qa_ref, TQ)
        oa = _softmax_pv([(_qk(qa, kax_ref[...]), vax_ref[...])], TQ, sink_ref=sink_ref)
        qb = _stack_mha(qb_ref, TQ)
        ob = _softmax_pv([(_qk(qb, kbx_ref[...]), vbx_ref[...])], TQ)
        qc = _stack_gqa(qc_ref, TQ)
        oc = _softmax_pv([(_qk(qc, kcx_ref[...]), vcx_ref[...])], TQ)
        store(o_ref, oa, ob, oc)

    blocks = [pl.ds(t * TQ, TQ) for t in range(BPS)]

    @pl.when(step < NQ // BPS)
    def _latent_queries():
        for t, rows in enumerate(blocks):
            latent_block(step * BPS + t, qa_all.at[rows], qc_all.at[rows], qb_all.at[rows],
                         bias_refs[t], o_all.at[rows])

    @pl.when(step >= NQ // BPS)
    def _context_queries():
        for rows in blocks:
            context_block(qa_all.at[rows], qc_all.at[rows], qb_all.at[rows], o_all.at[rows])


def _kv_cols():
    return [(128, COL_KA // 128), (128, COL_VA // 128), (128, COL_KC // 128), (128, COL_VC // 128),
            (256, COL_KB // 256), (256, COL_VB // 256)]


def _attention(qkv, sink, bias_tbl, tbl_idx, nb, with_ctx):
    ctx_blk0 = nb * SEQ // CTX_LEN
    lat_steps, ctx_steps = NQ // BPS, NQ_CTX // BPS
    n_steps = lat_steps + (ctx_steps if with_ctx else 0)
    qrow = lambda b, i: jnp.where(i < lat_steps, b * lat_steps + i,
                                  nb * lat_steps + b * ctx_steps + i - lat_steps)
    qspec = lambda w, cb: pl.BlockSpec((BPS * TQ, w), lambda b, i, t: (qrow(b, i), cb))
    lat = lambda w, cb: pl.BlockSpec((SEQ, w), lambda b, i, t: (b, cb))
    cxt = lambda w, cb: pl.BlockSpec((CTX_LEN, w), lambda b, i, t: (ctx_blk0 + b, cb))
    bias = lambda k: pl.BlockSpec((1, B_HEADS, TQ, NB_KEYS), lambda b, i, t: (t[BPS * i + k], 0, 0, 0))
    grid_spec = pltpu.PrefetchScalarGridSpec(
        num_scalar_prefetch=1,
        grid=(nb, n_steps),
        in_specs=[pl.BlockSpec(memory_space=pltpu.SMEM),
                  qspec(384, COL_QA // 384), qspec(384, COL_QC // 384), qspec(256, COL_QB // 256)]
                 + [lat(w, cb) for w, cb in _kv_cols()] + [cxt(w, cb) for w, cb in _kv_cols()]
                 + [bias(k) for k in range(BPS)],
        out_specs=pl.BlockSpec((BPS * TQ, D_MODEL), lambda b, i, t: (qrow(b, i), 0)),
    )
    return pl.pallas_call(
        _attn_kernel,
        out_shape=jax.ShapeDtypeStruct((nb * n_steps * BPS * TQ, D_MODEL), BF16),
        grid_spec=grid_spec,
        compiler_params=_compiler_params(2),
        name="attention",
    )(tbl_idx, sink, *([qkv] * 15), *([bias_tbl] * BPS))


def _ffn_kernel(o_ref, x_ref, gaa_ref, shf_ref, scf_ref, gaf_ref, g_ref,
                wo_ref, wg_ref, wu_ref, wd_ref, out_ref):
    a = jnp.dot(o_ref[...], wo_ref[...], preferred_element_type=F32)
    x1 = x_ref[...] + gaa_ref[0] * a
    h = _norm_modulate(x1, g_ref[...], scf_ref[0], shf_ref[0]).astype(BF16)
    y = None
    for c0, c1 in zip(FFN_SPLITS[:-1], FFN_SPLITS[1:]):
        cs = slice(c0, c1)
        u = jnp.dot(h, wg_ref[:, cs], preferred_element_type=F32)
        v = jnp.dot(h, wu_ref[:, cs], preferred_element_type=F32)
        act = (u / (1.0 + jnp.exp(-u)) * v).astype(BF16)
        yc = jnp.dot(act, wd_ref[cs, :], preferred_element_type=F32)
        y = yc if y is None else y + yc
    out_ref[...] = x1 + gaf_ref[0] * y


def _out_ffn(o, xx, mod3, g_ffn, w_out, w_gate, w_up, w_down, n_tiles, n_lat_tiles):
    tile = pl.BlockSpec((TM, D_MODEL), lambda i: (i, 0))
    return pl.pallas_call(
        _ffn_kernel,
        out_shape=jax.ShapeDtypeStruct((n_tiles * TM, D_MODEL), F32),
        grid=(n_tiles,),
        in_specs=[tile, tile] + [_mod_spec(k, n_lat_tiles) for k in (2, 3, 4, 5)]
                 + [_resident_spec((1, D_MODEL)), _resident_spec((D_MODEL, D_MODEL)),
                    _resident_spec((D_MODEL, FFN_HIDDEN)), _resident_spec((D_MODEL, FFN_HIDDEN)),
                    _resident_spec((FFN_HIDDEN, D_MODEL))],
        out_specs=tile,
        compiler_params=_compiler_params(1),
        name="out_ffn",
    )(o, xx, mod3, mod3, mod3, mod3, g_ffn, w_out, w_gate, w_up, w_down)


def kernel(x, c, ctx, c_ctx, w_mod, b_mod, norm_attn, norm_ffn, w_in, qk_norm, sink, rpb,
           w_out, w_gate, w_up, w_down):
    depth = w_mod.shape[0]
    nb = x.shape[0]
    assert x.shape == (nb, SEQ, D_MODEL) and ctx.shape == (nb, CTX_LEN, D_MODEL) and nb <= CTX_ROW
    assert (nb * CTX_LEN) % TM == 0
    n_lat_tiles = nb * LAT_TILES
    n_tiles = n_lat_tiles + nb * CTX_LEN // TM

    c8 = jnp.zeros((MOD_ROWS, D_MODEL), F32).at[:nb].set(c).at[CTX_ROW].set(c_ctx)
    mod = _modulation(c8, w_mod, b_mod)

    w_in_p = _permute(w_in, _in_perm(), 2).astype(BF16)
    w_out_p = _permute(w_out, _out_perm(), 1).astype(BF16)
    w_gate_b, w_up_b, w_down_b = w_gate.astype(BF16), w_up.astype(BF16), w_down.astype(BF16)
    cos_t, sin_t = _rope_tables()
    head_id = np.arange(2 * LANE) // HEAD_DIM
    bd = jnp.asarray((head_id[:, None] == head_id[None, :]) / HEAD_DIM, BF16)
    row_classes, tbl_idx = _nb_row_classes()
    bias_all = _bias_tables(rpb * LOG2E, row_classes)
    tbl_idx = jnp.asarray(np.concatenate([tbl_idx, np.full(NQ_CTX, tbl_idx[-1], np.int32)]))

    xx = jnp.concatenate([x.reshape(nb * SEQ, D_MODEL), ctx.reshape(nb * CTX_LEN, D_MODEL)], axis=0)
    for l in range(depth):
        last = l == depth - 1
        mod3 = mod[l].reshape(MOD_ROWS, 1, 6 * D_MODEL)
        qkv = _in_projection(xx, mod3, norm_attn[l][None], w_in_p[l], _gain_vector(qk_norm[l])[None],
                             bd, cos_t, sin_t, n_lat_tiles)
        o = _attention(qkv, sink[l], bias_all[l], tbl_idx, nb, not last)
        xx = _out_ffn(o, xx, mod3, norm_ffn[l][None], w_out_p[l], w_gate_b[l], w_up_b[l], w_down_b[l],
                      n_lat_tiles if last else n_tiles, n_lat_tiles)
    return xx.reshape(nb, SEQ, D_MODEL)
```

```python
import functools

import numpy as np
import jax
import jax.numpy as jnp
from jax import lax
from jax.experimental import pallas as pl
from jax.experimental.pallas import tpu as pltpu

D_MODEL = 1024
SEQ = 4096
CTX_LEN = 256
GRID_W = 64
ROWS = SEQ // GRID_W
HEAD_DIM = 64
A_HEADS = 6
B_HEADS = 4
C_HEADS = 6
GQA_G = 3
WINDOW = 128
NA_KH = 8
NA_KW = 16
FFN_HIDDEN = 2816
IN_WIDTH = 2048
ROPE_THETA = 10000.0
EPS = 1e-6
NEG_INF = -1e30
LOG2E = 1.4426950408889634
MOD_ROWS = 8
CTX_ROW = 4

LANE = 128
TM = 512
LAT_TILES = SEQ // TM
MXU_DIM = 256
FFN_SPLITS = (0, (FFN_HIDDEN // MXU_DIM + 1) // 2 * MXU_DIM, FFN_HIDDEN)
TQ = 128
NQ = SEQ // TQ
NQ_CTX = CTX_LEN // TQ
BPS = 2
Q_ROWS = TQ // GRID_W
NB_ROWS = Q_ROWS + NA_KH
NB_KEYS = NB_ROWS * GRID_W
A_SPAN = TQ + 2 * WINDOW
C_CHUNK = 1024
PIECE = 128
VMEM_LIMIT = 56 * 1024 * 1024

COL_QA, COL_QC, COL_KA, COL_KC = 0, 384, 768, 896
COL_QB, COL_KB = 1024, 1280
COL_VA, COL_VC, COL_VB = 1536, 1664, 1792
OCOL_A, OCOL_B, OCOL_C = 0, 384, 640

F32 = jnp.float32
BF16 = jnp.bfloat16
NT_DIMS = (((1,), (1,)), ((), ()))


def _gqa_cols(base):
    cols = np.empty(GQA_G * 2 * HEAD_DIM, np.int32)
    for j in range(GQA_G):
        for g in range(2):
            dst = j * LANE + g * HEAD_DIM
            cols[dst:dst + HEAD_DIM] = base + (g * GQA_G + j) * HEAD_DIM + np.arange(HEAD_DIM)
    return cols


def _in_perm():
    r = np.arange
    return np.concatenate([
        _gqa_cols(0),
        _gqa_cols(1408),
        384 + r(128),
        1792 + r(128),
        640 + r(256),
        896 + r(256),
        512 + r(128),
        1920 + r(128),
        1152 + r(256),
    ]).astype(np.int32)


def _out_perm():
    return np.concatenate([_gqa_cols(0), 384 + np.arange(256), _gqa_cols(640)]).astype(np.int32)


def _permute(w, perm, axis):
    cuts = [0] + [k for k in range(1, len(perm)) if perm[k] != perm[k - 1] + 1] + [len(perm)]
    parts = [lax.slice_in_dim(w, int(perm[a]), int(perm[b - 1]) + 1, axis=axis)
             for a, b in zip(cuts[:-1], cuts[1:])]
    return jnp.concatenate(parts, axis=axis)


def _gain_vector(qk):
    sc = HEAD_DIM ** -0.5 * LOG2E
    t = jnp.tile
    return jnp.concatenate([
        t(qk[0] * sc, 6), t(qk[4] * sc, 6), t(qk[1], 2), t(qk[5], 2),
        t(qk[2] * sc, 4), t(qk[3], 4), jnp.ones((512,), F32)])


def _rope_tables():
    t = jnp.arange(SEQ, dtype=jnp.int32)
    row = (t // GRID_W).astype(F32)
    col = (t % GRID_W).astype(F32)
    n_freq = HEAD_DIM // 4
    inv = ROPE_THETA ** (-jnp.arange(n_freq, dtype=F32) / n_freq)
    ang = jnp.concatenate([row[:, None] * inv[None, :], col[:, None] * inv[None, :]], axis=-1)
    cos, sin = jnp.cos(ang), jnp.sin(ang)
    cos = jnp.concatenate([cos, jnp.ones((TM, HEAD_DIM // 2), F32)], axis=0)
    sin = jnp.concatenate([sin, jnp.zeros((TM, HEAD_DIM // 2), F32)], axis=0)
    cos_t = jnp.tile(cos, (1, 2 * LANE // HEAD_DIM))
    sin_t = jnp.tile(jnp.concatenate([-sin, sin], axis=-1), (1, LANE // HEAD_DIM))
    return cos_t, sin_t


def _nb_row_classes():
    pats = []
    for i in range(NQ):
        ks = int(np.clip(Q_ROWS * i - NA_KH // 2, 0, ROWS - NB_ROWS))
        r = Q_ROWS * i + np.arange(Q_ROWS)[:, None]
        krow = ks + np.arange(NB_ROWS)[None, :]
        rs = np.clip(r - NA_KH // 2, 0, ROWS - NA_KH)
        valid = (krow >= rs) & (krow < rs + NA_KH)
        pats.append(np.where(valid, krow - r + (NA_KH - 1), -1))
    uniq, inv = np.unique(np.stack(pats).reshape(NQ, -1), axis=0, return_inverse=True)
    return uniq.reshape(-1, Q_ROWS, NB_ROWS), inv.reshape(-1).astype(np.int32)


def _bias_tables(rpb, row_classes):
    w = GRID_W
    pad = jnp.pad(rpb, ((0, 0), (0, 0), (0, 0), (w, w)), constant_values=NEG_INF)
    cols = jnp.stack([lax.slice_in_dim(pad, w + NA_KW - 1 - c, 2 * w + NA_KW - 1 - c, axis=3)
                      for c in range(w)], axis=3)
    c = np.arange(w)
    ws = np.clip(c - NA_KW // 2, 0, w - NA_KW)
    col_ok = (c[None, :] >= ws[:, None]) & (c[None, :] < ws[:, None] + NA_KW)
    cols = jnp.where(col_ok, cols, NEG_INF)
    neg = jnp.full(cols.shape[:2] + (w, w), NEG_INF, F32)
    tables = []
    for cls in row_classes:
        rows = [jnp.concatenate([cols[:, :, int(d)] if d >= 0 else neg for d in cls[a]], axis=-1)
                for a in range(Q_ROWS)]
        tables.append(jnp.concatenate(rows, axis=-2))
    return jnp.stack(tables, axis=1)


def _compiler_params(n_axes, flags=None):
    return pltpu.CompilerParams(dimension_semantics=("arbitrary",) * n_axes,
                                vmem_limit_bytes=VMEM_LIMIT, flags=flags)


def _mod_kernel(c_ref, w_ref, b_ref, o_ref):
    c = c_ref[...]
    a = (c / (1.0 + jnp.exp(-c))).astype(BF16)
    o_ref[0] = jnp.dot(a, w_ref[0].astype(BF16), preferred_element_type=F32) + b_ref[0]


def _modulation(c8, w_mod, b_mod):
    depth = w_mod.shape[0]
    tn = 2048
    return pl.pallas_call(
        _mod_kernel,
        out_shape=jax.ShapeDtypeStruct((depth, MOD_ROWS, 6 * D_MODEL), F32),
        grid=(depth, 6 * D_MODEL // tn),
        in_specs=[pl.BlockSpec((MOD_ROWS, D_MODEL), lambda l, n: (0, 0)),
                  pl.BlockSpec((1, D_MODEL, tn), lambda l, n: (l, 0, n)),
                  pl.BlockSpec((1, 1, tn), lambda l, n: (l, 0, n))],
        out_specs=pl.BlockSpec((1, MOD_ROWS, tn), lambda l, n: (l, 0, n)),
        compiler_params=_compiler_params(2),
        name="modulation",
    )(c8, w_mod, b_mod.reshape(depth, 1, 6 * D_MODEL))


def _mod_spec(layer, chunk, n_lat_tiles):
    return pl.BlockSpec(
        (None, 1, 1, D_MODEL),
        lambda i: (layer, jnp.where(i < n_lat_tiles, i // LAT_TILES, CTX_ROW), 0, chunk))


def _resident_spec(shape):
    return pl.BlockSpec(shape, lambda i: (0,) * len(shape), pipeline_mode=pl.Buffered(1))


def _layer_spec(layer, shape):
    return pl.BlockSpec((None,) + shape, lambda i: (layer,) + (0,) * len(shape),
                        pipeline_mode=pl.Buffered(1))


def _norm_modulate(x, g, scale, shift):
    ms = jnp.mean(x * x, axis=-1, keepdims=True)
    return x * lax.rsqrt(ms + EPS) * (g * (1.0 + scale)) + shift


def _inproj_kernel(x_ref, sh_ref, sc_ref, g_ref, w_ref, gv_ref, bd_ref, cos_ref, sin_ref, o_ref):
    h = _norm_modulate(x_ref[...], g_ref[...], sc_ref[0], sh_ref[0]).astype(BF16)
    lane = lax.broadcasted_iota(jnp.int32, (TM, LANE), 1)
    first_half = (lane % HEAD_DIM) < (HEAD_DIM // 2)
    cos = cos_ref[...]
    sin = sin_ref[...]
    cw = 2 * LANE
    proj = jnp.dot(h, w_ref[...], preferred_element_type=F32)
    for c in range(IN_WIDTH // cw):
        cs = slice(c * cw, (c + 1) * cw)
        p = proj[:, cs]
        if c * cw < COL_VA:
            ms = jnp.dot((p * p).astype(BF16), bd_ref[...], preferred_element_type=F32)
            p = p * lax.rsqrt(ms + EPS) * gv_ref[:, cs]
        if c * cw < COL_QB:
            halves = []
            for s in range(2):
                xh = p[:, s * LANE:(s + 1) * LANE]
                partner = jnp.where(first_half,
                                    pltpu.roll(xh, LANE - HEAD_DIM // 2, 1),
                                    pltpu.roll(xh, HEAD_DIM // 2, 1))
                halves.append(xh * cos + partner * sin)
            p = jnp.concatenate(halves, axis=1)
        o_ref[:, cs] = p.astype(BF16)


def _in_projection(layer, xx, mod4, g_attn, w_in, gvec, bd, cos_t, sin_t, n_lat_tiles):
    ntok = xx.shape[0]
    rope_spec = pl.BlockSpec(
        (TM, LANE), lambda i: (jnp.where(i < n_lat_tiles, i % LAT_TILES, LAT_TILES), 0))
    return pl.pallas_call(
        _inproj_kernel,
        out_shape=jax.ShapeDtypeStruct((ntok, IN_WIDTH), BF16),
        grid=(ntok // TM,),
        in_specs=[pl.BlockSpec((TM, D_MODEL), lambda i: (i, 0)),
                  _mod_spec(layer, 0, n_lat_tiles), _mod_spec(layer, 1, n_lat_tiles),
                  _layer_spec(layer, (1, D_MODEL)), _layer_spec(layer, (D_MODEL, IN_WIDTH)),
                  _layer_spec(layer, (1, IN_WIDTH)), _resident_spec((2 * LANE, 2 * LANE)),
                  rope_spec, rope_spec],
        out_specs=pl.BlockSpec((TM, IN_WIDTH), lambda i: (i, 0)),
        compiler_params=_compiler_params(1),
        name="in_projection",
    )(xx, mod4, mod4, g_attn, w_in, gvec, bd, cos_t, sin_t)


def _stack_gqa(q_ref, nq):
    low = lax.broadcasted_iota(jnp.int32, (nq, LANE), 1) < HEAD_DIM
    blocks = []
    for g in range(2):
        for j in range(GQA_G):
            qj = q_ref[:, j * LANE:(j + 1) * LANE]
            blocks.append(jnp.where(low if g == 0 else jnp.logical_not(low), qj, jnp.zeros_like(qj)))
    return jnp.concatenate(blocks, axis=0)


def _unstack_gqa(o, nq):
    low = lax.broadcasted_iota(jnp.int32, (nq, LANE), 1) < HEAD_DIM
    return [jnp.where(low, o[j * nq:(j + 1) * nq], o[(GQA_G + j) * nq:(GQA_G + j + 1) * nq])
            for j in range(GQA_G)]


def _stack_mha(q_ref, nq):
    q = q_ref[...]
    head = lax.broadcasted_iota(jnp.int32, (nq, B_HEADS * HEAD_DIM), 1) // HEAD_DIM
    return jnp.concatenate([jnp.where(head == h, q, jnp.zeros_like(q)) for h in range(B_HEADS)], axis=0)


def _unstack_mha(o, nq):
    head = lax.broadcasted_iota(jnp.int32, (nq, B_HEADS * HEAD_DIM), 1) // HEAD_DIM
    out = jnp.where(head == 0, o[0:nq], 0.0)
    for h in range(1, B_HEADS):
        out = jnp.where(head == h, o[h * nq:(h + 1) * nq], out)
    return out


def _qk(q, k):
    return lax.dot_general(q, k, NT_DIMS, preferred_element_type=F32)


def _softmax_pv(parts, nq, sink_ref=None, adjust=None):
    rows = parts[0][0].shape[0]
    width = parts[0][1].shape[1]
    mxu_sums = width == LANE
    probs = [[] for _ in parts]
    dens = []
    for r0 in range(0, rows, PIECE):
        sl = [s[r0:r0 + PIECE] for s, _ in parts]
        if adjust is not None:
            sl[0] = adjust(sl[0], r0 // nq, r0 % nq)
        m = None
        for sp in sl:
            sm = jnp.max(sp, axis=-1, keepdims=True)
            m = sm if m is None else jnp.maximum(m, sm)
        den = None
        if sink_ref is not None:
            sink = sink_ref[r0 // nq] * LOG2E
            m = jnp.maximum(m, sink)
            den = jnp.exp2(sink - m)
        for k, sp in enumerate(sl):
            p = jnp.exp2(sp - m)
            if not mxu_sums:
                ps = jnp.sum(p, axis=-1, keepdims=True)
                den = ps if den is None else den + ps
            probs[k].append(p.astype(BF16))
        if den is not None:
            dens.append(den)
    acc = None
    for k, (_, v) in enumerate(parts):
        if mxu_sums:
            v = jnp.concatenate([v, jnp.ones(v.shape, BF16)], axis=1)
        pv = jnp.dot(jnp.concatenate(probs[k], axis=0), v, preferred_element_type=F32)
        acc = pv if acc is None else acc + pv
    if not mxu_sums:
        return acc * (1.0 / jnp.concatenate(dens, axis=0))
    den = acc[:, width:]
    if dens:
        den = den + jnp.concatenate(dens, axis=0)
    return acc[:, :width] * (1.0 / den)


def _global_attention(q, k_ref, v_ref, kx_ref, vx_ref):
    chunks = [(k_ref, v_ref, slice(k0, k0 + C_CHUNK)) for k0 in range(0, SEQ, C_CHUNK)]
    chunks.append((kx_ref, vx_ref, slice(0, CTX_LEN)))
    m = acc = None
    for kr, vr, ks in chunks:
        s = _qk(q, kr[ks, :])
        v_ext = jnp.concatenate([vr[ks, :], jnp.ones((ks.stop - ks.start, LANE), BF16)], axis=1)
        sm = jnp.max(s, axis=-1, keepdims=True)
        m_new = sm if m is None else jnp.maximum(m, sm)
        pv = jnp.dot(jnp.exp2(s - m_new).astype(BF16), v_ext, preferred_element_type=F32)
        acc = pv if m is None else acc * jnp.exp2(m - m_new) + pv
        m = m_new
    return acc[:, :LANE] * (1.0 / acc[:, LANE:])


def _attn_kernel(layer, tbl_ref, sink_all, qa_all, qc_all, qb_all,
                 ka_ref, va_ref, kc_ref, vc_ref, kb_ref, vb_ref,
                 kax_ref, vax_ref, kcx_ref, vcx_ref, kbx_ref, vbx_ref, *rest):
    del tbl_ref
    sink_ref = sink_all.at[layer]
    bias_refs, o_all = rest[:BPS], rest[BPS]
    step = pl.program_id(1)

    def store(o_ref, oa, ob, oc):
        for j, blk in enumerate(_unstack_gqa(oa, TQ)):
            o_ref[:, OCOL_A + j * LANE:OCOL_A + (j + 1) * LANE] = blk.astype(BF16)
        o_ref[:, OCOL_B:OCOL_C] = _unstack_mha(ob, TQ).astype(BF16)
        for j, blk in enumerate(_unstack_gqa(oc, TQ)):
            o_ref[:, OCOL_C + j * LANE:OCOL_C + (j + 1) * LANE] = blk.astype(BF16)

    def latent_block(i, qa_ref, qc_ref, qb_ref, bias_ref, o_ref):
        qa = _stack_gqa(qa_ref, TQ)
        start = pl.multiple_of(jnp.clip(i * TQ - WINDOW, 0, SEQ - A_SPAN), LANE)
        na = A_SPAN + CTX_LEN
        col = lax.broadcasted_iota(jnp.int32, (TQ, na), 1)
        qpos = i * TQ + lax.broadcasted_iota(jnp.int32, (TQ, na), 0)
        valid = (jnp.abs(qpos - (start + col)) <= WINDOW) | (col >= A_SPAN)
        k_all = jnp.concatenate([ka_ref[pl.ds(start, A_SPAN), :], kax_ref[...]], axis=0)
        v_all = jnp.concatenate([va_ref[pl.ds(start, A_SPAN), :], vax_ref[...]], axis=0)
        oa = _softmax_pv([(_qk(qa, k_all), v_all)], TQ, sink_ref=sink_ref,
                         adjust=lambda sp, n, q0: jnp.where(valid[q0:q0 + PIECE], sp, NEG_INF))

        qb = _stack_mha(qb_ref, TQ)
        krow0 = jnp.clip(Q_ROWS * i - NA_KH // 2, 0, ROWS - NB_ROWS)
        kstart = pl.multiple_of(krow0 * GRID_W, GRID_W)
        k_all = jnp.concatenate([kb_ref[pl.ds(kstart, NB_KEYS), :], kbx_ref[...]], axis=0)
        v_all = jnp.concatenate([vb_ref[pl.ds(kstart, NB_KEYS), :], vbx_ref[...]], axis=0)
        ob = _softmax_pv(
            [(_qk(qb, k_all), v_all)], TQ,
            adjust=lambda sp, n, q0: jnp.concatenate(
                [sp[:, :NB_KEYS] + bias_ref[0, n, q0:q0 + PIECE, :], sp[:, NB_KEYS:]], axis=1))

        oc = _global_attention(_stack_gqa(qc_ref, TQ), kc_ref, vc_ref, kcx_ref, vcx_ref)
        store(o_ref, oa, ob, oc)

    def context_block(qa_ref, qc_ref, qb_ref, o_ref):
        qa = _stack_gqa(qa_ref, TQ)
        oa = _softmax_pv([(_qk(qa, kax_ref[...]), vax_ref[...])], TQ, sink_ref=sink_ref)
        qb = _stack_mha(qb_ref, TQ)
        ob = _softmax_pv([(_qk(qb, kbx_ref[...]), vbx_ref[...])], TQ)
        qc = _stack_gqa(qc_ref, TQ)
        oc = _softmax_pv([(_qk(qc, kcx_ref[...]), vcx_ref[...])], TQ)
        store(o_ref, oa, ob, oc)

    blocks = [pl.ds(t * TQ, TQ) for t in range(BPS)]

    @pl.when(step < NQ // BPS)
    def _latent_queries():
        for t, rows in enumerate(blocks):
            latent_block(step * BPS + t, qa_all.at[rows], qc_all.at[rows], qb_all.at[rows],
                         bias_refs[t], o_all.at[rows])

    @pl.when(step >= NQ // BPS)
    def _context_queries():
        for rows in blocks:
            context_block(qa_all.at[rows], qc_all.at[rows], qb_all.at[rows], o_all.at[rows])


def _kv_cols():
    return [(128, COL_KA // 128), (128, COL_VA // 128), (128, COL_KC // 128), (128, COL_VC // 128),
            (256, COL_KB // 256), (256, COL_VB // 256)]


def _attention(layer, qkv, sink, bias_tbl, tbl_idx, nb, with_ctx):
    ctx_blk0 = nb * SEQ // CTX_LEN
    lat_steps, ctx_steps = NQ // BPS, NQ_CTX // BPS
    n_steps = lat_steps + (ctx_steps if with_ctx else 0)
    qrow = lambda b, i: jnp.where(i < lat_steps, b * lat_steps + i,
                                  nb * lat_steps + b * ctx_steps + i - lat_steps)
    qspec = lambda w, cb: pl.BlockSpec((BPS * TQ, w), lambda b, i, t: (qrow(b, i), cb))
    lat = lambda w, cb: pl.BlockSpec((SEQ, w), lambda b, i, t: (b, cb))
    cxt = lambda w, cb: pl.BlockSpec((CTX_LEN, w), lambda b, i, t: (ctx_blk0 + b, cb))
    bias = lambda k: pl.BlockSpec((None, 1, B_HEADS, TQ, NB_KEYS),
                                  lambda b, i, t: (layer, t[BPS * i + k], 0, 0, 0))
    grid_spec = pltpu.PrefetchScalarGridSpec(
        num_scalar_prefetch=1,
        grid=(nb, n_steps),
        in_specs=[pl.BlockSpec(memory_space=pltpu.SMEM),
                  qspec(384, COL_QA // 384), qspec(384, COL_QC // 384), qspec(256, COL_QB // 256)]
                 + [lat(w, cb) for w, cb in _kv_cols()] + [cxt(w, cb) for w, cb in _kv_cols()]
                 + [bias(k) for k in range(BPS)],
        out_specs=pl.BlockSpec((BPS * TQ, D_MODEL), lambda b, i, t: (qrow(b, i), 0)),
    )
    return pl.pallas_call(
        functools.partial(_attn_kernel, layer),
        out_shape=jax.ShapeDtypeStruct((nb * n_steps * BPS * TQ, D_MODEL), BF16),
        grid_spec=grid_spec,
        compiler_params=_compiler_params(2),
        name="attention",
    )(tbl_idx, sink, *([qkv] * 15), *([bias_tbl] * BPS))


def _ffn_kernel(o_ref, x_ref, gaa_ref, shf_ref, scf_ref, gaf_ref, g_ref,
                wo_ref, wg_ref, wu_ref, wd_ref, out_ref):
    a = jnp.dot(o_ref[...], wo_ref[...], preferred_element_type=F32)
    x1 = x_ref[...] + gaa_ref[0] * a
    h = _norm_modulate(x1, g_ref[...], scf_ref[0], shf_ref[0]).astype(BF16)
    y = None
    for c0, c1 in zip(FFN_SPLITS[:-1], FFN_SPLITS[1:]):
        cs = slice(c0, c1)
        u = jnp.dot(h, wg_ref[:, cs], preferred_element_type=F32)
        v = jnp.dot(h, wu_ref[:, cs], preferred_element_type=F32)
        act = (u / (1.0 + jnp.exp(-u)) * v).astype(BF16)
        yc = jnp.dot(act, wd_ref[cs, :], preferred_element_type=F32)
        y = yc if y is None else y + yc
    out_ref[...] = x1 + gaf_ref[0] * y


def _out_ffn(layer, o, xx, mod4, g_ffn, w_out, w_gate, w_up, w_down, n_tiles, n_lat_tiles):
    tile = pl.BlockSpec((TM, D_MODEL), lambda i: (i, 0))
    return pl.pallas_call(
        _ffn_kernel,
        out_shape=jax.ShapeDtypeStruct((n_tiles * TM, D_MODEL), F32),
        grid=(n_tiles,),
        in_specs=[tile, tile] + [_mod_spec(layer, k, n_lat_tiles) for k in (2, 3, 4, 5)]
                 + [_layer_spec(layer, (1, D_MODEL)), _layer_spec(layer, (D_MODEL, D_MODEL)),
                    _layer_spec(layer, (D_MODEL, FFN_HIDDEN)),
                    _layer_spec(layer, (D_MODEL, FFN_HIDDEN)),
                    _layer_spec(layer, (FFN_HIDDEN, D_MODEL))],
        out_specs=tile,
        compiler_params=_compiler_params(1),
        name="out_ffn",
    )(o, xx, mod4, mod4, mod4, mod4, g_ffn, w_out, w_gate, w_up, w_down)


def kernel(x, c, ctx, c_ctx, w_mod, b_mod, norm_attn, norm_ffn, w_in, qk_norm, sink, rpb,
           w_out, w_gate, w_up, w_down):
    depth = w_mod.shape[0]
    nb = x.shape[0]
    assert x.shape == (nb, SEQ, D_MODEL) and ctx.shape == (nb, CTX_LEN, D_MODEL) and nb <= CTX_ROW
    assert (nb * CTX_LEN) % TM == 0
    n_lat_tiles = nb * LAT_TILES
    n_tiles = n_lat_tiles + nb * CTX_LEN // TM

    c8 = jnp.zeros((MOD_ROWS, D_MODEL), F32).at[:nb].set(c).at[CTX_ROW].set(c_ctx)
    mod4 = _modulation(c8, w_mod, b_mod).reshape(depth, MOD_ROWS, 1, 6 * D_MODEL)
    g_attn, g_ffn = norm_attn[:, None, :], norm_ffn[:, None, :]
    gvec = jax.vmap(_gain_vector)(qk_norm)[:, None, :]

    w_in_p = _permute(w_in.astype(BF16), _in_perm(), 2)
    w_out_p = _permute(w_out.astype(BF16), _out_perm(), 1)
    w_gate_b, w_up_b, w_down_b = w_gate.astype(BF16), w_up.astype(BF16), w_down.astype(BF16)
    cos_t, sin_t = _rope_tables()
    head_id = np.arange(2 * LANE) // HEAD_DIM
    bd = jnp.asarray((head_id[:, None] == head_id[None, :]) / HEAD_DIM, BF16)
    row_classes, tbl_idx = _nb_row_classes()
    bias_all = _bias_tables(rpb * LOG2E, row_classes)
    tbl_idx = jnp.asarray(np.concatenate([tbl_idx, np.full(NQ_CTX, tbl_idx[-1], np.int32)]))

    xx = jnp.concatenate([x.reshape(nb * SEQ, D_MODEL), ctx.reshape(nb * CTX_LEN, D_MODEL)], axis=0)
    for l in range(depth):
        last = l == depth - 1
        qkv = _in_projection(l, xx, mod4, g_attn, w_in_p, gvec, bd, cos_t, sin_t, n_lat_tiles)
        o = _attention(l, qkv, sink, bias_all, tbl_idx, nb, not last)
        xx = _out_ffn(l, o, xx, mod4, g_ffn, w_out_p, w_gate_b, w_up_b, w_down_b,
                      n_lat_tiles if last else n_tiles, n_lat_tiles)
    return xx.reshape(nb, SEQ, D_MODEL)
```

```python
import functools

import numpy as np
import jax
import jax.numpy as jnp
from jax import lax
from jax.experimental import pallas as pl
from jax.experimental.pallas import tpu as pltpu

D_MODEL = 1024
SEQ = 4096
CTX_LEN = 256
GRID_W = 64
ROWS = SEQ // GRID_W
HEAD_DIM = 64
A_HEADS = 6
B_HEADS = 4
C_HEADS = 6
GQA_G = 3
WINDOW = 128
NA_KH = 8
NA_KW = 16
FFN_HIDDEN = 2816
IN_WIDTH = 2048
ROPE_THETA = 10000.0
EPS = 1e-6
NEG_INF = -1e30
LOG2E = 1.4426950408889634
MOD_ROWS = 8
CTX_ROW = 4

LANE = 128
TM = 512
LAT_TILES = SEQ // TM
PROJ_GROUPS = 4
MXU_DIM = 256
FFN_SPLITS = (0, (FFN_HIDDEN // MXU_DIM + 1) // 2 * MXU_DIM, FFN_HIDDEN)
TQ = 128
NQ = SEQ // TQ
NQ_CTX = CTX_LEN // TQ
BPS = 2
Q_ROWS = TQ // GRID_W
NB_ROWS = Q_ROWS + NA_KH
NB_KEYS = NB_ROWS * GRID_W
A_SPAN = TQ + 2 * WINDOW
C_CHUNK = 1024
PIECE = 128
VMEM_LIMIT = 56 * 1024 * 1024

COL_QA, COL_QC, COL_KA, COL_KC = 0, 384, 768, 896
COL_QB, COL_KB = 1024, 1280
COL_VA, COL_VC, COL_VB = 1536, 1664, 1792
OCOL_A, OCOL_B, OCOL_C = 0, 384, 640

F32 = jnp.float32
BF16 = jnp.bfloat16
NT_DIMS = (((1,), (1,)), ((), ()))


def _gqa_cols(base):
    cols = np.empty(GQA_G * 2 * HEAD_DIM, np.int32)
    for j in range(GQA_G):
        for g in range(2):
            dst = j * LANE + g * HEAD_DIM
            cols[dst:dst + HEAD_DIM] = base + (g * GQA_G + j) * HEAD_DIM + np.arange(HEAD_DIM)
    return cols


def _in_perm():
    r = np.arange
    return np.concatenate([
        _gqa_cols(0),
        _gqa_cols(1408),
        384 + r(128),
        1792 + r(128),
        640 + r(256),
        896 + r(256),
        512 + r(128),
        1920 + r(128),
        1152 + r(256),
    ]).astype(np.int32)


def _out_perm():
    return np.concatenate([_gqa_cols(0), 384 + np.arange(256), _gqa_cols(640)]).astype(np.int32)


def _permute(w, perm, axis):
    cuts = [0] + [k for k in range(1, len(perm)) if perm[k] != perm[k - 1] + 1] + [len(perm)]
    parts = [lax.slice_in_dim(w, int(perm[a]), int(perm[b - 1]) + 1, axis=axis)
             for a, b in zip(cuts[:-1], cuts[1:])]
    return jnp.concatenate(parts, axis=axis)


def _gain_vector(qk):
    sc = HEAD_DIM ** -0.5 * LOG2E
    t = jnp.tile
    return jnp.concatenate([
        t(qk[0] * sc, 6), t(qk[4] * sc, 6), t(qk[1], 2), t(qk[5], 2),
        t(qk[2] * sc, 4), t(qk[3], 4), jnp.ones((512,), F32)])


def _rope_tables():
    t = jnp.arange(SEQ, dtype=jnp.int32)
    row = (t // GRID_W).astype(F32)
    col = (t % GRID_W).astype(F32)
    n_freq = HEAD_DIM // 4
    inv = ROPE_THETA ** (-jnp.arange(n_freq, dtype=F32) / n_freq)
    ang = jnp.concatenate([row[:, None] * inv[None, :], col[:, None] * inv[None, :]], axis=-1)
    cos, sin = jnp.cos(ang), jnp.sin(ang)
    cos = jnp.concatenate([cos, jnp.ones((TM, HEAD_DIM // 2), F32)], axis=0)
    sin = jnp.concatenate([sin, jnp.zeros((TM, HEAD_DIM // 2), F32)], axis=0)
    cos_t = jnp.tile(cos, (1, 2 * LANE // HEAD_DIM))
    sin_t = jnp.tile(jnp.concatenate([-sin, sin], axis=-1), (1, LANE // HEAD_DIM))
    return cos_t, sin_t


def _nb_row_classes():
    pats = []
    for i in range(NQ):
        ks = int(np.clip(Q_ROWS * i - NA_KH // 2, 0, ROWS - NB_ROWS))
        r = Q_ROWS * i + np.arange(Q_ROWS)[:, None]
        krow = ks + np.arange(NB_ROWS)[None, :]
        rs = np.clip(r - NA_KH // 2, 0, ROWS - NA_KH)
        valid = (krow >= rs) & (krow < rs + NA_KH)
        pats.append(np.where(valid, krow - r + (NA_KH - 1), -1))
    uniq, inv = np.unique(np.stack(pats).reshape(NQ, -1), axis=0, return_inverse=True)
    return uniq.reshape(-1, Q_ROWS, NB_ROWS), inv.reshape(-1).astype(np.int32)


def _bias_tables(rpb, row_classes):
    w = GRID_W
    pad = jnp.pad(rpb, ((0, 0), (0, 0), (0, 0), (w, w)), constant_values=NEG_INF)
    cols = jnp.stack([lax.slice_in_dim(pad, w + NA_KW - 1 - c, 2 * w + NA_KW - 1 - c, axis=3)
                      for c in range(w)], axis=3)
    c = np.arange(w)
    ws = np.clip(c - NA_KW // 2, 0, w - NA_KW)
    col_ok = (c[None, :] >= ws[:, None]) & (c[None, :] < ws[:, None] + NA_KW)
    cols = jnp.where(col_ok, cols, NEG_INF)
    neg = jnp.full(cols.shape[:2] + (w, w), NEG_INF, F32)
    tables = []
    for cls in row_classes:
        rows = [jnp.concatenate([cols[:, :, int(d)] if d >= 0 else neg for d in cls[a]], axis=-1)
                for a in range(Q_ROWS)]
        tables.append(jnp.concatenate(rows, axis=-2))
    return jnp.stack(tables, axis=1)


def _compiler_params(n_axes, flags=None):
    return pltpu.CompilerParams(dimension_semantics=("arbitrary",) * n_axes,
                                vmem_limit_bytes=VMEM_LIMIT, flags=flags)


def _mod_kernel(c_ref, w_ref, b_ref, o_ref):
    c = c_ref[...]
    a = (c / (1.0 + jnp.exp(-c))).astype(BF16)
    o_ref[0] = jnp.dot(a, w_ref[0].astype(BF16), preferred_element_type=F32) + b_ref[0]


def _modulation(c8, w_mod, b_mod):
    depth = w_mod.shape[0]
    tn = 2048
    return pl.pallas_call(
        _mod_kernel,
        out_shape=jax.ShapeDtypeStruct((depth, MOD_ROWS, 6 * D_MODEL), F32),
        grid=(depth, 6 * D_MODEL // tn),
        in_specs=[pl.BlockSpec((MOD_ROWS, D_MODEL), lambda l, n: (0, 0)),
                  pl.BlockSpec((1, D_MODEL, tn), lambda l, n: (l, 0, n)),
                  pl.BlockSpec((1, 1, tn), lambda l, n: (l, 0, n))],
        out_specs=pl.BlockSpec((1, MOD_ROWS, tn), lambda l, n: (l, 0, n)),
        compiler_params=_compiler_params(2),
        name="modulation",
    )(c8, w_mod, b_mod.reshape(depth, 1, 6 * D_MODEL))


def _mod_spec(layer, chunk, n_lat_tiles):
    return pl.BlockSpec(
        (None, 1, 1, D_MODEL),
        lambda i: (layer, jnp.where(i < n_lat_tiles, i // LAT_TILES, CTX_ROW), 0, chunk))


def _resident_spec(shape):
    return pl.BlockSpec(shape, lambda i: (0,) * len(shape), pipeline_mode=pl.Buffered(1))


def _layer_spec(layer, shape):
    return pl.BlockSpec((None,) + shape, lambda i: (layer,) + (0,) * len(shape),
                        pipeline_mode=pl.Buffered(1))


def _norm_modulate(x, g, scale, shift):
    ms = jnp.mean(x * x, axis=-1, keepdims=True)
    return x * lax.rsqrt(ms + EPS) * (g * (1.0 + scale)) + shift


def _inproj_kernel(x_ref, sh_ref, sc_ref, g_ref, w_ref, gv_ref, bd_ref, cos_ref, sin_ref, o_ref):
    h = _norm_modulate(x_ref[...], g_ref[...], sc_ref[0], sh_ref[0]).astype(BF16)
    lane = lax.broadcasted_iota(jnp.int32, (TM, LANE), 1)
    first_half = (lane % HEAD_DIM) < (HEAD_DIM // 2)
    cos = cos_ref[...]
    sin = sin_ref[...]
    cw = 2 * LANE
    gw = IN_WIDTH // PROJ_GROUPS
    for c in range(IN_WIDTH // cw):
        cs = slice(c * cw, (c + 1) * cw)
        if (c * cw) % gw == 0:
            proj = jnp.dot(h, w_ref[:, c * cw:c * cw + gw], preferred_element_type=F32)
        p = proj[:, (c * cw) % gw:(c * cw) % gw + cw]
        if c * cw < COL_VA:
            ms = jnp.dot((p * p).astype(BF16), bd_ref[...], preferred_element_type=F32)
            p = p * lax.rsqrt(ms + EPS) * gv_ref[:, cs]
        if c * cw < COL_QB:
            halves = []
            for s in range(2):
                xh = p[:, s * LANE:(s + 1) * LANE]
                partner = jnp.where(first_half,
                                    pltpu.roll(xh, LANE - HEAD_DIM // 2, 1),
                                    pltpu.roll(xh, HEAD_DIM // 2, 1))
                halves.append(xh * cos + partner * sin)
            p = jnp.concatenate(halves, axis=1)
        o_ref[:, cs] = p.astype(BF16)


def _in_projection(layer, xx, mod4, g_attn, w_in, gvec, bd, cos_t, sin_t, n_lat_tiles):
    ntok = xx.shape[0]
    rope_spec = pl.BlockSpec(
        (TM, LANE), lambda i: (jnp.where(i < n_lat_tiles, i % LAT_TILES, LAT_TILES), 0))
    return pl.pallas_call(
        _inproj_kernel,
        out_shape=jax.ShapeDtypeStruct((ntok, IN_WIDTH), BF16),
        grid=(ntok // TM,),
        in_specs=[pl.BlockSpec((TM, D_MODEL), lambda i: (i, 0)),
                  _mod_spec(layer, 0, n_lat_tiles), _mod_spec(layer, 1, n_lat_tiles),
                  _layer_spec(layer, (1, D_MODEL)), _layer_spec(layer, (D_MODEL, IN_WIDTH)),
                  _layer_spec(layer, (1, IN_WIDTH)), _resident_spec((2 * LANE, 2 * LANE)),
                  rope_spec, rope_spec],
        out_specs=pl.BlockSpec((TM, IN_WIDTH), lambda i: (i, 0)),
        compiler_params=_compiler_params(1),
        name="in_projection",
    )(xx, mod4, mod4, g_attn, w_in, gvec, bd, cos_t, sin_t)


def _stack_gqa(q_ref, nq):
    low = lax.broadcasted_iota(jnp.int32, (nq, LANE), 1) < HEAD_DIM
    blocks = []
    for g in range(2):
        for j in range(GQA_G):
            qj = q_ref[:, j * LANE:(j + 1) * LANE]
            blocks.append(jnp.where(low if g == 0 else jnp.logical_not(low), qj, jnp.zeros_like(qj)))
    return jnp.concatenate(blocks, axis=0)


def _unstack_gqa(o, nq):
    low = lax.broadcasted_iota(jnp.int32, (nq, LANE), 1) < HEAD_DIM
    return [jnp.where(low, o[j * nq:(j + 1) * nq], o[(GQA_G + j) * nq:(GQA_G + j + 1) * nq])
            for j in range(GQA_G)]


def _stack_mha(q_ref, nq):
    q = q_ref[...]
    head = lax.broadcasted_iota(jnp.int32, (nq, B_HEADS * HEAD_DIM), 1) // HEAD_DIM
    return jnp.concatenate([jnp.where(head == h, q, jnp.zeros_like(q)) for h in range(B_HEADS)], axis=0)


def _unstack_mha(o, nq):
    head = lax.broadcasted_iota(jnp.int32, (nq, B_HEADS * HEAD_DIM), 1) // HEAD_DIM
    out = jnp.where(head == 0, o[0:nq], 0.0)
    for h in range(1, B_HEADS):
        out = jnp.where(head == h, o[h * nq:(h + 1) * nq], out)
    return out


def _qk(q, k):
    return lax.dot_general(q, k, NT_DIMS, preferred_element_type=F32)


def _softmax_pv(parts, nq, sink_ref=None, adjust=None):
    rows = parts[0][0].shape[0]
    width = parts[0][1].shape[1]
    mxu_sums = width == LANE
    probs = [[] for _ in parts]
    dens = []
    for r0 in range(0, rows, PIECE):
        sl = [s[r0:r0 + PIECE] for s, _ in parts]
        if adjust is not None:
            sl[0] = adjust(sl[0], r0 // nq, r0 % nq)
        m = None
        for sp in sl:
            sm = jnp.max(sp, axis=-1, keepdims=True)
            m = sm if m is None else jnp.maximum(m, sm)
        den = None
        if sink_ref is not None:
            sink = sink_ref[r0 // nq] * LOG2E
            m = jnp.maximum(m, sink)
            den = jnp.exp2(sink - m)
        for k, sp in enumerate(sl):
            p = jnp.exp2(sp - m)
            if not mxu_sums:
                ps = jnp.sum(p, axis=-1, keepdims=True)
                den = ps if den is None else den + ps
            probs[k].append(p.astype(BF16))
        if den is not None:
            dens.append(den)
    acc = None
    for k, (_, v) in enumerate(parts):
        if mxu_sums:
            v = jnp.concatenate([v, jnp.ones(v.shape, BF16)], axis=1)
        pv = jnp.dot(jnp.concatenate(probs[k], axis=0), v, preferred_element_type=F32)
        acc = pv if acc is None else acc + pv
    if not mxu_sums:
        return acc * (1.0 / jnp.concatenate(dens, axis=0))
    den = acc[:, width:]
    if dens:
        den = den + jnp.concatenate(dens, axis=0)
    return acc[:, :width] * (1.0 / den)


def _global_attention(q, k_ref, v_ref, kx_ref, vx_ref):
    chunks = [(k_ref, v_ref, slice(k0, k0 + C_CHUNK)) for k0 in range(0, SEQ, C_CHUNK)]
    chunks.append((kx_ref, vx_ref, slice(0, CTX_LEN)))
    m = acc = None
    for kr, vr, ks in chunks:
        s = _qk(q, kr[ks, :])
        v_ext = jnp.concatenate([vr[ks, :], jnp.ones((ks.stop - ks.start, LANE), BF16)], axis=1)
        sm = jnp.max(s, axis=-1, keepdims=True)
        m_new = sm if m is None else jnp.maximum(m, sm)
        pv = jnp.dot(jnp.exp2(s - m_new).astype(BF16), v_ext, preferred_element_type=F32)
        acc = pv if m is None else acc * jnp.exp2(m - m_new) + pv
        m = m_new
    return acc[:, :LANE] * (1.0 / acc[:, LANE:])


def _attn_kernel(layer, tbl_ref, sink_all, qa_all, qc_all, qb_all,
                 ka_ref, va_ref, kc_ref, vc_ref, kb_ref, vb_ref,
                 kax_ref, vax_ref, kcx_ref, vcx_ref, kbx_ref, vbx_ref, *rest):
    del tbl_ref
    sink_ref = sink_all.at[layer]
    bias_refs, o_all = rest[:BPS], rest[BPS]
    step = pl.program_id(1)

    def store(o_ref, oa, ob, oc):
        for j, blk in enumerate(_unstack_gqa(oa, TQ)):
            o_ref[:, OCOL_A + j * LANE:OCOL_A + (j + 1) * LANE] = blk.astype(BF16)
        o_ref[:, OCOL_B:OCOL_C] = _unstack_mha(ob, TQ).astype(BF16)
        for j, blk in enumerate(_unstack_gqa(oc, TQ)):
            o_ref[:, OCOL_C + j * LANE:OCOL_C + (j + 1) * LANE] = blk.astype(BF16)

    def latent_block(i, qa_ref, qc_ref, qb_ref, bias_ref, o_ref):
        qa = _stack_gqa(qa_ref, TQ)
        start = pl.multiple_of(jnp.clip(i * TQ - WINDOW, 0, SEQ - A_SPAN), LANE)
        na = A_SPAN + CTX_LEN
        col = lax.broadcasted_iota(jnp.int32, (TQ, na), 1)
        qpos = i * TQ + lax.broadcasted_iota(jnp.int32, (TQ, na), 0)
        valid = (jnp.abs(qpos - (start + col)) <= WINDOW) | (col >= A_SPAN)
        k_all = jnp.concatenate([ka_ref[pl.ds(start, A_SPAN), :], kax_ref[...]], axis=0)
        v_all = jnp.concatenate([va_ref[pl.ds(start, A_SPAN), :], vax_ref[...]], axis=0)
        oa = _softmax_pv([(_qk(qa, k_all), v_all)], TQ, sink_ref=sink_ref,
                         adjust=lambda sp, n, q0: jnp.where(valid[q0:q0 + PIECE], sp, NEG_INF))

        qb = _stack_mha(qb_ref, TQ)
        krow0 = jnp.clip(Q_ROWS * i - NA_KH // 2, 0, ROWS - NB_ROWS)
        kstart = pl.multiple_of(krow0 * GRID_W, GRID_W)
        k_all = jnp.concatenate([kb_ref[pl.ds(kstart, NB_KEYS), :], kbx_ref[...]], axis=0)
        v_all = jnp.concatenate([vb_ref[pl.ds(kstart, NB_KEYS), :], vbx_ref[...]], axis=0)
        ob = _softmax_pv(
            [(_qk(qb, k_all), v_all)], TQ,
            adjust=lambda sp, n, q0: jnp.concatenate(
                [sp[:, :NB_KEYS] + bias_ref[0, n, q0:q0 + PIECE, :], sp[:, NB_KEYS:]], axis=1))

        oc = _global_attention(_stack_gqa(qc_ref, TQ), kc_ref, vc_ref, kcx_ref, vcx_ref)
        store(o_ref, oa, ob, oc)

    def context_block(qa_ref, qc_ref, qb_ref, o_ref):
        qa = _stack_gqa(qa_ref, TQ)
        oa = _softmax_pv([(_qk(qa, kax_ref[...]), vax_ref[...])], TQ, sink_ref=sink_ref)
        qb = _stack_mha(qb_ref, TQ)
        ob = _softmax_pv([(_qk(qb, kbx_ref[...]), vbx_ref[...])], TQ)
        qc = _stack_gqa(qc_ref, TQ)
        oc = _softmax_pv([(_qk(qc, kcx_ref[...]), vcx_ref[...])], TQ)
        store(o_ref, oa, ob, oc)

    blocks = [pl.ds(t * TQ, TQ) for t in range(BPS)]

    @pl.when(step < NQ // BPS)
    def _latent_queries():
        for t, rows in enumerate(blocks):
            latent_block(step * BPS + t, qa_all.at[rows], qc_all.at[rows], qb_all.at[rows],
                         bias_refs[t], o_all.at[rows])

    @pl.when(step >= NQ // BPS)
    def _context_queries():
        for rows in blocks:
            context_block(qa_all.at[rows], qc_all.at[rows], qb_all.at[rows], o_all.at[rows])


def _kv_cols():
    return [(128, COL_KA // 128), (128, COL_VA // 128), (128, COL_KC // 128), (128, COL_VC // 128),
            (256, COL_KB // 256), (256, COL_VB // 256)]


def _attention(layer, qkv, sink, bias_tbl, tbl_idx, nb, with_ctx):
    ctx_blk0 = nb * SEQ // CTX_LEN
    lat_steps, ctx_steps = NQ // BPS, NQ_CTX // BPS
    n_steps = lat_steps + (ctx_steps if with_ctx else 0)
    qrow = lambda b, i: jnp.where(i < lat_steps, b * lat_steps + i,
                                  nb * lat_steps + b * ctx_steps + i - lat_steps)
    qspec = lambda w, cb: pl.BlockSpec((BPS * TQ, w), lambda b, i, t: (qrow(b, i), cb))
    lat = lambda w, cb: pl.BlockSpec((SEQ, w), lambda b, i, t: (b, cb))
    cxt = lambda w, cb: pl.BlockSpec((CTX_LEN, w), lambda b, i, t: (ctx_blk0 + b, cb))
    bias = lambda k: pl.BlockSpec((None, 1, B_HEADS, TQ, NB_KEYS),
                                  lambda b, i, t: (layer, t[BPS * i + k], 0, 0, 0))
    grid_spec = pltpu.PrefetchScalarGridSpec(
        num_scalar_prefetch=1,
        grid=(nb, n_steps),
        in_specs=[pl.BlockSpec(memory_space=pltpu.SMEM),
                  qspec(384, COL_QA // 384), qspec(384, COL_QC // 384), qspec(256, COL_QB // 256)]
                 + [lat(w, cb) for w, cb in _kv_cols()] + [cxt(w, cb) for w, cb in _kv_cols()]
                 + [bias(k) for k in range(BPS)],
        out_specs=pl.BlockSpec((BPS * TQ, D_MODEL), lambda b, i, t: (qrow(b, i), 0)),
    )
    return pl.pallas_call(
        functools.partial(_attn_kernel, layer),
        out_shape=jax.ShapeDtypeStruct((nb * n_steps * BPS * TQ, D_MODEL), BF16),
        grid_spec=grid_spec,
        compiler_params=_compiler_params(2),
        name="attention",
    )(tbl_idx, sink, *([qkv] * 15), *([bias_tbl] * BPS))


def _ffn_kernel(o_ref, x_ref, gaa_ref, shf_ref, scf_ref, gaf_ref, g_ref,
                wo_ref, wg_ref, wu_ref, wd_ref, out_ref):
    a = jnp.dot(o_ref[...], wo_ref[...], preferred_element_type=F32)
    x1 = x_ref[...] + gaa_ref[0] * a
    h = _norm_modulate(x1, g_ref[...], scf_ref[0], shf_ref[0]).astype(BF16)
    y = None
    for c0, c1 in zip(FFN_SPLITS[:-1], FFN_SPLITS[1:]):
        cs = slice(c0, c1)
        u = jnp.dot(h, wg_ref[:, cs], preferred_element_type=F32)
        v = jnp.dot(h, wu_ref[:, cs], preferred_element_type=F32)
        act = (u / (1.0 + jnp.exp(-u)) * v).astype(BF16)
        yc = jnp.dot(act, wd_ref[cs, :], preferred_element_type=F32)
        y = yc if y is None else y + yc
    out_ref[...] = x1 + gaf_ref[0] * y


def _out_ffn(layer, o, xx, mod4, g_ffn, w_out, w_gate, w_up, w_down, n_tiles, n_lat_tiles):
    tile = pl.BlockSpec((TM, D_MODEL), lambda i: (i, 0))
    return pl.pallas_call(
        _ffn_kernel,
        out_shape=jax.ShapeDtypeStruct((n_tiles * TM, D_MODEL), F32),
        grid=(n_tiles,),
        in_specs=[tile, tile] + [_mod_spec(layer, k, n_lat_tiles) for k in (2, 3, 4, 5)]
                 + [_layer_spec(layer, (1, D_MODEL)), _layer_spec(layer, (D_MODEL, D_MODEL)),
                    _layer_spec(layer, (D_MODEL, FFN_HIDDEN)),
                    _layer_spec(layer, (D_MODEL, FFN_HIDDEN)),
                    _layer_spec(layer, (FFN_HIDDEN, D_MODEL))],
        out_specs=tile,
        compiler_params=_compiler_params(1),
        name="out_ffn",
    )(o, xx, mod4, mod4, mod4, mod4, g_ffn, w_out, w_gate, w_up, w_down)


def kernel(x, c, ctx, c_ctx, w_mod, b_mod, norm_attn, norm_ffn, w_in, qk_norm, sink, rpb,
           w_out, w_gate, w_up, w_down):
    depth = w_mod.shape[0]
    nb = x.shape[0]
    assert x.shape == (nb, SEQ, D_MODEL) and ctx.shape == (nb, CTX_LEN, D_MODEL) and nb <= CTX_ROW
    assert (nb * CTX_LEN) % TM == 0
    n_lat_tiles = nb * LAT_TILES
    n_tiles = n_lat_tiles + nb * CTX_LEN // TM

    c8 = jnp.zeros((MOD_ROWS, D_MODEL), F32).at[:nb].set(c).at[CTX_ROW].set(c_ctx)
    mod4 = _modulation(c8, w_mod, b_mod).reshape(depth, MOD_ROWS, 1, 6 * D_MODEL)
    g_attn, g_ffn = norm_attn[:, None, :], norm_ffn[:, None, :]
    gvec = jax.vmap(_gain_vector)(qk_norm)[:, None, :]

    w_in_p = _permute(w_in.astype(BF16), _in_perm(), 2)
    w_out_p = _permute(w_out.astype(BF16), _out_perm(), 1)
    w_gate_b, w_up_b, w_down_b = w_gate.astype(BF16), w_up.astype(BF16), w_down.astype(BF16)
    cos_t, sin_t = _rope_tables()
    head_id = np.arange(2 * LANE) // HEAD_DIM
    bd = jnp.asarray((head_id[:, None] == head_id[None, :]) / HEAD_DIM, BF16)
    row_classes, tbl_idx = _nb_row_classes()
    bias_all = _bias_tables(rpb * LOG2E, row_classes)
    tbl_idx = jnp.asarray(np.concatenate([tbl_idx, np.full(NQ_CTX, tbl_idx[-1], np.int32)]))

    xx = jnp.concatenate([x.reshape(nb * SEQ, D_MODEL), ctx.reshape(nb * CTX_LEN, D_MODEL)], axis=0)
    for l in range(depth):
        last = l == depth - 1
        qkv = _in_projection(l, xx, mod4, g_attn, w_in_p, gvec, bd, cos_t, sin_t, n_lat_tiles)
        o = _attention(l, qkv, sink, bias_all, tbl_idx, nb, not last)
        xx = _out_ffn(l, o, xx, mod4, g_ffn, w_out_p, w_gate_b, w_up_b, w_down_b,
                      n_lat_tiles if last else n_tiles, n_lat_tiles)
    return xx.reshape(nb, SEQ, D_MODEL)
```

```python
import functools

import numpy as np
import jax
import jax.numpy as jnp
from jax import lax
from jax.experimental import pallas as pl
from jax.experimental.pallas import tpu as pltpu

D_MODEL = 1024
SEQ = 4096
CTX_LEN = 256
GRID_W = 64
ROWS = SEQ // GRID_W
HEAD_DIM = 64
A_HEADS = 6
B_HEADS = 4
C_HEADS = 6
GQA_G = 3
WINDOW = 128
NA_KH = 8
NA_KW = 16
FFN_HIDDEN = 2816
IN_WIDTH = 2048
ROPE_THETA = 10000.0
EPS = 1e-6
NEG_INF = -1e30
LOG2E = 1.4426950408889634
MOD_ROWS = 8
CTX_ROW = 4

LANE = 128
TM = 512
LAT_TILES = SEQ // TM
PROJ_GROUPS = 4
MXU_DIM = 256
FFN_SPLITS = (0, (FFN_HIDDEN // MXU_DIM + 1) // 2 * MXU_DIM, FFN_HIDDEN)
TQ = 128
NQ = SEQ // TQ
NQ_CTX = CTX_LEN // TQ
BPS = 2
Q_ROWS = TQ // GRID_W
NB_ROWS = Q_ROWS + NA_KH
NB_KEYS = NB_ROWS * GRID_W
A_SPAN = TQ + 2 * WINDOW
C_CHUNK = 1024
PIECE = 128
VMEM_LIMIT = 56 * 1024 * 1024

COL_QA, COL_QC, COL_KA, COL_KC = 0, 384, 768, 896
COL_QB, COL_KB = 1024, 1280
COL_VA, COL_VC, COL_VB = 1536, 1664, 1792
OCOL_A, OCOL_B, OCOL_C = 0, 384, 640

F32 = jnp.float32
BF16 = jnp.bfloat16
NT_DIMS = (((1,), (1,)), ((), ()))


def _gqa_cols(base):
    cols = np.empty(GQA_G * 2 * HEAD_DIM, np.int32)
    for j in range(GQA_G):
        for g in range(2):
            dst = j * LANE + g * HEAD_DIM
            cols[dst:dst + HEAD_DIM] = base + (g * GQA_G + j) * HEAD_DIM + np.arange(HEAD_DIM)
    return cols


def _in_perm():
    r = np.arange
    return np.concatenate([
        _gqa_cols(0),
        _gqa_cols(1408),
        384 + r(128),
        1792 + r(128),
        640 + r(256),
        896 + r(256),
        512 + r(128),
        1920 + r(128),
        1152 + r(256),
    ]).astype(np.int32)


def _out_perm():
    return np.concatenate([_gqa_cols(0), 384 + np.arange(256), _gqa_cols(640)]).astype(np.int32)


def _permute(w, perm, axis):
    cuts = [0] + [k for k in range(1, len(perm)) if perm[k] != perm[k - 1] + 1] + [len(perm)]
    parts = [lax.slice_in_dim(w, int(perm[a]), int(perm[b - 1]) + 1, axis=axis)
             for a, b in zip(cuts[:-1], cuts[1:])]
    return jnp.concatenate(parts, axis=axis)


def _gain_vector(qk):
    sc = HEAD_DIM ** -0.5 * LOG2E
    t = jnp.tile
    return jnp.concatenate([
        t(qk[0] * sc, 6), t(qk[4] * sc, 6), t(qk[1], 2), t(qk[5], 2),
        t(qk[2] * sc, 4), t(qk[3], 4), jnp.ones((512,), F32)])


def _rope_tables():
    t = jnp.arange(SEQ, dtype=jnp.int32)
    row = (t // GRID_W).astype(F32)
    col = (t % GRID_W).astype(F32)
    n_freq = HEAD_DIM // 4
    inv = ROPE_THETA ** (-jnp.arange(n_freq, dtype=F32) / n_freq)
    ang = jnp.concatenate([row[:, None] * inv[None, :], col[:, None] * inv[None, :]], axis=-1)
    cos, sin = jnp.cos(ang), jnp.sin(ang)
    cos = jnp.concatenate([cos, jnp.ones((TM, HEAD_DIM // 2), F32)], axis=0)
    sin = jnp.concatenate([sin, jnp.zeros((TM, HEAD_DIM // 2), F32)], axis=0)
    cos_t = jnp.tile(cos, (1, 2 * LANE // HEAD_DIM))
    sin_t = jnp.tile(jnp.concatenate([-sin, sin], axis=-1), (1, LANE // HEAD_DIM))
    return cos_t, sin_t


def _nb_row_classes():
    pats = []
    for i in range(NQ):
        ks = int(np.clip(Q_ROWS * i - NA_KH // 2, 0, ROWS - NB_ROWS))
        r = Q_ROWS * i + np.arange(Q_ROWS)[:, None]
        krow = ks + np.arange(NB_ROWS)[None, :]
        rs = np.clip(r - NA_KH // 2, 0, ROWS - NA_KH)
        valid = (krow >= rs) & (krow < rs + NA_KH)
        pats.append(np.where(valid, krow - r + (NA_KH - 1), -1))
    uniq, inv = np.unique(np.stack(pats).reshape(NQ, -1), axis=0, return_inverse=True)
    return uniq.reshape(-1, Q_ROWS, NB_ROWS), inv.reshape(-1).astype(np.int32)


def _bias_tables(rpb, row_classes):
    w = GRID_W
    a = jnp.pad(rpb, ((0, 0), (0, 0), (0, 0), (w - NA_KW, w - NA_KW)), constant_values=NEG_INF)
    a = jnp.roll(a, -(w - 1), axis=3)
    cols = jnp.tile(a, (1, 1, 1, w))[..., :w * (2 * w - 2)]
    cols = cols.reshape(rpb.shape[:3] + (w, 2 * w - 2))[..., :w]
    c = np.arange(w)
    ws = np.clip(c - NA_KW // 2, 0, w - NA_KW)
    col_ok = (c[None, :] >= ws[:, None]) & (c[None, :] < ws[:, None] + NA_KW)
    cols = jnp.where(col_ok, cols, NEG_INF)
    neg = jnp.full(cols.shape[:2] + (w, w), NEG_INF, F32)
    tables = []
    for cls in row_classes:
        rows = [jnp.concatenate([cols[:, :, int(d)] if d >= 0 else neg for d in cls[a]], axis=-1)
                for a in range(Q_ROWS)]
        tables.append(jnp.concatenate(rows, axis=-2))
    return jnp.stack(tables, axis=1)


def _compiler_params(n_axes, flags=None):
    return pltpu.CompilerParams(dimension_semantics=("arbitrary",) * n_axes,
                                vmem_limit_bytes=VMEM_LIMIT, flags=flags)


def _mod_kernel(c_ref, w_ref, b_ref, o_ref):
    c = c_ref[...]
    a = (c / (1.0 + jnp.exp(-c))).astype(BF16)
    o_ref[0] = jnp.dot(a, w_ref[0].astype(BF16), preferred_element_type=F32) + b_ref[0]


def _modulation(c8, w_mod, b_mod):
    depth = w_mod.shape[0]
    tn = 2048
    return pl.pallas_call(
        _mod_kernel,
        out_shape=jax.ShapeDtypeStruct((depth, MOD_ROWS, 6 * D_MODEL), F32),
        grid=(depth, 6 * D_MODEL // tn),
        in_specs=[pl.BlockSpec((MOD_ROWS, D_MODEL), lambda l, n: (0, 0)),
                  pl.BlockSpec((1, D_MODEL, tn), lambda l, n: (l, 0, n)),
                  pl.BlockSpec((1, 1, tn), lambda l, n: (l, 0, n))],
        out_specs=pl.BlockSpec((1, MOD_ROWS, tn), lambda l, n: (l, 0, n)),
        compiler_params=_compiler_params(2),
        name="modulation",
    )(c8, w_mod, b_mod.reshape(depth, 1, 6 * D_MODEL))


def _mod_spec(layer, chunk, n_lat_tiles):
    return pl.BlockSpec(
        (None, 1, 1, D_MODEL),
        lambda i: (layer, jnp.where(i < n_lat_tiles, i // LAT_TILES, CTX_ROW), 0, chunk))


def _resident_spec(shape):
    return pl.BlockSpec(shape, lambda i: (0,) * len(shape), pipeline_mode=pl.Buffered(1))


def _layer_spec(layer, shape):
    return pl.BlockSpec((None,) + shape, lambda i: (layer,) + (0,) * len(shape),
                        pipeline_mode=pl.Buffered(1))


def _norm_modulate(x, g, scale, shift):
    ms = jnp.mean(x * x, axis=-1, keepdims=True)
    return x * lax.rsqrt(ms + EPS) * (g * (1.0 + scale)) + shift


def _inproj_kernel(x_ref, sh_ref, sc_ref, g_ref, w_ref, gv_ref, bd_ref, cos_ref, sin_ref, o_ref):
    h = _norm_modulate(x_ref[...], g_ref[...], sc_ref[0], sh_ref[0]).astype(BF16)
    lane = lax.broadcasted_iota(jnp.int32, (TM, LANE), 1)
    first_half = (lane % HEAD_DIM) < (HEAD_DIM // 2)
    cos = cos_ref[...]
    sin = sin_ref[...]
    cw = 2 * LANE
    gw = IN_WIDTH // PROJ_GROUPS
    for c in range(IN_WIDTH // cw):
        cs = slice(c * cw, (c + 1) * cw)
        if (c * cw) % gw == 0:
            proj = jnp.dot(h, w_ref[:, c * cw:c * cw + gw], preferred_element_type=F32)
        p = proj[:, (c * cw) % gw:(c * cw) % gw + cw]
        if c * cw < COL_VA:
            ms = jnp.dot((p * p).astype(BF16), bd_ref[...], preferred_element_type=F32)
            p = p * lax.rsqrt(ms + EPS) * gv_ref[:, cs]
        if c * cw < COL_QB:
            halves = []
            for s in range(2):
                xh = p[:, s * LANE:(s + 1) * LANE]
                partner = jnp.where(first_half,
                                    pltpu.roll(xh, LANE - HEAD_DIM // 2, 1),
                                    pltpu.roll(xh, HEAD_DIM // 2, 1))
                halves.append(xh * cos + partner * sin)
            p = jnp.concatenate(halves, axis=1)
        o_ref[:, cs] = p.astype(BF16)


def _in_projection(layer, xx, mod4, g_attn, w_in, gvec, bd, cos_t, sin_t, n_lat_tiles):
    ntok = xx.shape[0]
    rope_spec = pl.BlockSpec(
        (TM, LANE), lambda i: (jnp.where(i < n_lat_tiles, i % LAT_TILES, LAT_TILES), 0))
    return pl.pallas_call(
        _inproj_kernel,
        out_shape=jax.ShapeDtypeStruct((ntok, IN_WIDTH), BF16),
        grid=(ntok // TM,),
        in_specs=[pl.BlockSpec((TM, D_MODEL), lambda i: (i, 0)),
                  _mod_spec(layer, 0, n_lat_tiles), _mod_spec(layer, 1, n_lat_tiles),
                  _layer_spec(layer, (1, D_MODEL)), _layer_spec(layer, (D_MODEL, IN_WIDTH)),
                  _layer_spec(layer, (1, IN_WIDTH)), _resident_spec((2 * LANE, 2 * LANE)),
                  rope_spec, rope_spec],
        out_specs=pl.BlockSpec((TM, IN_WIDTH), lambda i: (i, 0)),
        compiler_params=_compiler_params(1),
        name="in_projection",
    )(xx, mod4, mod4, g_attn, w_in, gvec, bd, cos_t, sin_t)


def _stack_gqa(q_ref, nq):
    low = lax.broadcasted_iota(jnp.int32, (nq, LANE), 1) < HEAD_DIM
    blocks = []
    for g in range(2):
        for j in range(GQA_G):
            qj = q_ref[:, j * LANE:(j + 1) * LANE]
            blocks.append(jnp.where(low if g == 0 else jnp.logical_not(low), qj, jnp.zeros_like(qj)))
    return jnp.concatenate(blocks, axis=0)


def _unstack_gqa(o, nq):
    low = lax.broadcasted_iota(jnp.int32, (nq, LANE), 1) < HEAD_DIM
    return [jnp.where(low, o[j * nq:(j + 1) * nq], o[(GQA_G + j) * nq:(GQA_G + j + 1) * nq])
            for j in range(GQA_G)]


def _stack_mha(q_ref, nq):
    q = q_ref[...]
    head = lax.broadcasted_iota(jnp.int32, (nq, B_HEADS * HEAD_DIM), 1) // HEAD_DIM
    return jnp.concatenate([jnp.where(head == h, q, jnp.zeros_like(q)) for h in range(B_HEADS)], axis=0)


def _unstack_mha(o, nq):
    head = lax.broadcasted_iota(jnp.int32, (nq, B_HEADS * HEAD_DIM), 1) // HEAD_DIM
    out = jnp.where(head == 0, o[0:nq], 0.0)
    for h in range(1, B_HEADS):
        out = jnp.where(head == h, o[h * nq:(h + 1) * nq], out)
    return out


def _qk(q, k):
    return lax.dot_general(q, k, NT_DIMS, preferred_element_type=F32)


def _softmax_pv(parts, nq, sink_ref=None, adjust=None):
    rows = parts[0][0].shape[0]
    width = parts[0][1].shape[1]
    mxu_sums = width == LANE
    probs = [[] for _ in parts]
    dens = []
    for r0 in range(0, rows, PIECE):
        sl = [s[r0:r0 + PIECE] for s, _ in parts]
        if adjust is not None:
            sl[0] = adjust(sl[0], r0 // nq, r0 % nq)
        m = None
        for sp in sl:
            sm = jnp.max(sp, axis=-1, keepdims=True)
            m = sm if m is None else jnp.maximum(m, sm)
        den = None
        if sink_ref is not None:
            sink = sink_ref[r0 // nq] * LOG2E
            m = jnp.maximum(m, sink)
            den = jnp.exp2(sink - m)
        for k, sp in enumerate(sl):
            p = jnp.exp2(sp - m)
            if not mxu_sums:
                ps = jnp.sum(p, axis=-1, keepdims=True)
                den = ps if den is None else den + ps
            probs[k].append(p.astype(BF16))
        if den is not None:
            dens.append(den)
    acc = None
    for k, (_, v) in enumerate(parts):
        if mxu_sums:
            v = jnp.concatenate([v, jnp.ones(v.shape, BF16)], axis=1)
        pv = jnp.dot(jnp.concatenate(probs[k], axis=0), v, preferred_element_type=F32)
        acc = pv if acc is None else acc + pv
    if not mxu_sums:
        return acc * (1.0 / jnp.concatenate(dens, axis=0))
    den = acc[:, width:]
    if dens:
        den = den + jnp.concatenate(dens, axis=0)
    return acc[:, :width] * (1.0 / den)


def _global_attention(q, k_ref, v_ref, kx_ref, vx_ref):
    chunks = [(k_ref, v_ref, slice(k0, k0 + C_CHUNK)) for k0 in range(0, SEQ, C_CHUNK)]
    chunks.append((kx_ref, vx_ref, slice(0, CTX_LEN)))
    m = acc = None
    for kr, vr, ks in chunks:
        s = _qk(q, kr[ks, :])
        v_ext = jnp.concatenate([vr[ks, :], jnp.ones((ks.stop - ks.start, LANE), BF16)], axis=1)
        sm = jnp.max(s, axis=-1, keepdims=True)
        m_new = sm if m is None else jnp.maximum(m, sm)
        pv = jnp.dot(jnp.exp2(s - m_new).astype(BF16), v_ext, preferred_element_type=F32)
        acc = pv if m is None else acc * jnp.exp2(m - m_new) + pv
        m = m_new
    return acc[:, :LANE] * (1.0 / acc[:, LANE:])


def _attn_kernel(layer, tbl_ref, sink_all, qa_all, qc_all, qb_all,
                 ka_ref, va_ref, kc_ref, vc_ref, kb_ref, vb_ref,
                 kax_ref, vax_ref, kcx_ref, vcx_ref, kbx_ref, vbx_ref, *rest):
    del tbl_ref
    sink_ref = sink_all.at[layer]
    bias_refs, o_all = rest[:BPS], rest[BPS]
    step = pl.program_id(1)

    def store(o_ref, oa, ob, oc):
        for j, blk in enumerate(_unstack_gqa(oa, TQ)):
            o_ref[:, OCOL_A + j * LANE:OCOL_A + (j + 1) * LANE] = blk.astype(BF16)
        o_ref[:, OCOL_B:OCOL_C] = _unstack_mha(ob, TQ).astype(BF16)
        for j, blk in enumerate(_unstack_gqa(oc, TQ)):
            o_ref[:, OCOL_C + j * LANE:OCOL_C + (j + 1) * LANE] = blk.astype(BF16)

    def latent_block(i, qa_ref, qc_ref, qb_ref, bias_ref, o_ref):
        qa = _stack_gqa(qa_ref, TQ)
        start = pl.multiple_of(jnp.clip(i * TQ - WINDOW, 0, SEQ - A_SPAN), LANE)
        na = A_SPAN + CTX_LEN
        col = lax.broadcasted_iota(jnp.int32, (TQ, na), 1)
        qpos = i * TQ + lax.broadcasted_iota(jnp.int32, (TQ, na), 0)
        valid = (jnp.abs(qpos - (start + col)) <= WINDOW) | (col >= A_SPAN)
        k_all = jnp.concatenate([ka_ref[pl.ds(start, A_SPAN), :], kax_ref[...]], axis=0)
        v_all = jnp.concatenate([va_ref[pl.ds(start, A_SPAN), :], vax_ref[...]], axis=0)
        oa = _softmax_pv([(_qk(qa, k_all), v_all)], TQ, sink_ref=sink_ref,
                         adjust=lambda sp, n, q0: jnp.where(valid[q0:q0 + PIECE], sp, NEG_INF))

        qb = _stack_mha(qb_ref, TQ)
        krow0 = jnp.clip(Q_ROWS * i - NA_KH // 2, 0, ROWS - NB_ROWS)
        kstart = pl.multiple_of(krow0 * GRID_W, GRID_W)
        k_all = jnp.concatenate([kb_ref[pl.ds(kstart, NB_KEYS), :], kbx_ref[...]], axis=0)
        v_all = jnp.concatenate([vb_ref[pl.ds(kstart, NB_KEYS), :], vbx_ref[...]], axis=0)
        ob = _softmax_pv(
            [(_qk(qb, k_all), v_all)], TQ,
            adjust=lambda sp, n, q0: jnp.concatenate(
                [sp[:, :NB_KEYS] + bias_ref[0, n, q0:q0 + PIECE, :], sp[:, NB_KEYS:]], axis=1))

        oc = _global_attention(_stack_gqa(qc_ref, TQ), kc_ref, vc_ref, kcx_ref, vcx_ref)
        store(o_ref, oa, ob, oc)

    def context_block(qa_ref, qc_ref, qb_ref, o_ref):
        qa = _stack_gqa(qa_ref, TQ)
        oa = _softmax_pv([(_qk(qa, kax_ref[...]), vax_ref[...])], TQ, sink_ref=sink_ref)
        qb = _stack_mha(qb_ref, TQ)
        ob = _softmax_pv([(_qk(qb, kbx_ref[...]), vbx_ref[...])], TQ)
        qc = _stack_gqa(qc_ref, TQ)
        oc = _softmax_pv([(_qk(qc, kcx_ref[...]), vcx_ref[...])], TQ)
        store(o_ref, oa, ob, oc)

    blocks = [pl.ds(t * TQ, TQ) for t in range(BPS)]

    @pl.when(step < NQ // BPS)
    def _latent_queries():
        for t, rows in enumerate(blocks):
            latent_block(step * BPS + t, qa_all.at[rows], qc_all.at[rows], qb_all.at[rows],
                         bias_refs[t], o_all.at[rows])

    @pl.when(step >= NQ // BPS)
    def _context_queries():
        for rows in blocks:
            context_block(qa_all.at[rows], qc_all.at[rows], qb_all.at[rows], o_all.at[rows])


def _kv_cols():
    return [(128, COL_KA // 128), (128, COL_VA // 128), (128, COL_KC // 128), (128, COL_VC // 128),
            (256, COL_KB // 256), (256, COL_VB // 256)]


def _attention(layer, qkv, sink, bias_tbl, tbl_idx, nb, with_ctx):
    ctx_blk0 = nb * SEQ // CTX_LEN
    lat_steps, ctx_steps = NQ // BPS, NQ_CTX // BPS
    n_steps = lat_steps + (ctx_steps if with_ctx else 0)
    qrow = lambda b, i: jnp.where(i < lat_steps, b * lat_steps + i,
                                  nb * lat_steps + b * ctx_steps + i - lat_steps)
    qspec = lambda w, cb: pl.BlockSpec((BPS * TQ, w), lambda b, i, t: (qrow(b, i), cb))
    lat = lambda w, cb: pl.BlockSpec((SEQ, w), lambda b, i, t: (b, cb))
    cxt = lambda w, cb: pl.BlockSpec((CTX_LEN, w), lambda b, i, t: (ctx_blk0 + b, cb))
    bias = lambda k: pl.BlockSpec((None, 1, B_HEADS, TQ, NB_KEYS),
                                  lambda b, i, t: (layer, t[BPS * i + k], 0, 0, 0))
    grid_spec = pltpu.PrefetchScalarGridSpec(
        num_scalar_prefetch=1,
        grid=(nb, n_steps),
        in_specs=[pl.BlockSpec(memory_space=pltpu.SMEM),
                  qspec(384, COL_QA // 384), qspec(384, COL_QC // 384), qspec(256, COL_QB // 256)]
                 + [lat(w, cb) for w, cb in _kv_cols()] + [cxt(w, cb) for w, cb in _kv_cols()]
                 + [bias(k) for k in range(BPS)],
        out_specs=pl.BlockSpec((BPS * TQ, D_MODEL), lambda b, i, t: (qrow(b, i), 0)),
    )
    return pl.pallas_call(
        functools.partial(_attn_kernel, layer),
        out_shape=jax.ShapeDtypeStruct((nb * n_steps * BPS * TQ, D_MODEL), BF16),
        grid_spec=grid_spec,
        compiler_params=_compiler_params(2),
        name="attention",
    )(tbl_idx, sink, *([qkv] * 15), *([bias_tbl] * BPS))


def _ffn_kernel(o_ref, x_ref, gaa_ref, shf_ref, scf_ref, gaf_ref, g_ref,
                wo_ref, wg_ref, wu_ref, wd_ref, out_ref):
    a = jnp.dot(o_ref[...], wo_ref[...], preferred_element_type=F32)
    x1 = x_ref[...] + gaa_ref[0] * a
    h = _norm_modulate(x1, g_ref[...], scf_ref[0], shf_ref[0]).astype(BF16)
    y = None
    for c0, c1 in zip(FFN_SPLITS[:-1], FFN_SPLITS[1:]):
        cs = slice(c0, c1)
        u = jnp.dot(h, wg_ref[:, cs], preferred_element_type=F32)
        v = jnp.dot(h, wu_ref[:, cs], preferred_element_type=F32)
        act = (u / (1.0 + jnp.exp(-u)) * v).astype(BF16)
        yc = jnp.dot(act, wd_ref[cs, :], preferred_element_type=F32)
        y = yc if y is None else y + yc
    out_ref[...] = x1 + gaf_ref[0] * y


def _out_ffn(layer, o, xx, mod4, g_ffn, w_out, w_gate, w_up, w_down, n_tiles, n_lat_tiles):
    tile = pl.BlockSpec((TM, D_MODEL), lambda i: (i, 0))
    return pl.pallas_call(
        _ffn_kernel,
        out_shape=jax.ShapeDtypeStruct((n_tiles * TM, D_MODEL), F32),
        grid=(n_tiles,),
        in_specs=[tile, tile] + [_mod_spec(layer, k, n_lat_tiles) for k in (2, 3, 4, 5)]
                 + [_layer_spec(layer, (1, D_MODEL)), _layer_spec(layer, (D_MODEL, D_MODEL)),
                    _layer_spec(layer, (D_MODEL, FFN_HIDDEN)),
                    _layer_spec(layer, (D_MODEL, FFN_HIDDEN)),
                    _layer_spec(layer, (FFN_HIDDEN, D_MODEL))],
        out_specs=tile,
        compiler_params=_compiler_params(1),
        name="out_ffn",
    )(o, xx, mod4, mod4, mod4, mod4, g_ffn, w_out, w_gate, w_up, w_down)


def kernel(x, c, ctx, c_ctx, w_mod, b_mod, norm_attn, norm_ffn, w_in, qk_norm, sink, rpb,
           w_out, w_gate, w_up, w_down):
    depth = w_mod.shape[0]
    nb = x.shape[0]
    assert x.shape == (nb, SEQ, D_MODEL) and ctx.shape == (nb, CTX_LEN, D_MODEL) and nb <= CTX_ROW
    assert (nb * CTX_LEN) % TM == 0
    n_lat_tiles = nb * LAT_TILES
    n_tiles = n_lat_tiles + nb * CTX_LEN // TM

    c8 = jnp.zeros((MOD_ROWS, D_MODEL), F32).at[:nb].set(c).at[CTX_ROW].set(c_ctx)
    mod4 = _modulation(c8, w_mod, b_mod).reshape(depth, MOD_ROWS, 1, 6 * D_MODEL)
    g_attn, g_ffn = norm_attn[:, None, :], norm_ffn[:, None, :]
    gvec = jax.vmap(_gain_vector)(qk_norm)[:, None, :]

    w_in_p = _permute(w_in.astype(BF16), _in_perm(), 2)
    w_out_p = _permute(w_out.astype(BF16), _out_perm(), 1)
    w_gate_b, w_up_b, w_down_b = w_gate.astype(BF16), w_up.astype(BF16), w_down.astype(BF16)
    cos_t, sin_t = _rope_tables()
    head_id = np.arange(2 * LANE) // HEAD_DIM
    bd = jnp.asarray((head_id[:, None] == head_id[None, :]) / HEAD_DIM, BF16)
    row_classes, tbl_idx = _nb_row_classes()
    bias_all = _bias_tables(rpb * LOG2E, row_classes)
    tbl_idx = jnp.asarray(np.concatenate([tbl_idx, np.full(NQ_CTX, tbl_idx[-1], np.int32)]))

    xx = jnp.concatenate([x.reshape(nb * SEQ, D_MODEL), ctx.reshape(nb * CTX_LEN, D_MODEL)], axis=0)
    for l in range(depth):
        last = l == depth - 1
        qkv = _in_projection(l, xx, mod4, g_attn, w_in_p, gvec, bd, cos_t, sin_t, n_lat_tiles)
        o = _attention(l, qkv, sink, bias_all, tbl_idx, nb, not last)
        xx = _out_ffn(l, o, xx, mod4, g_ffn, w_out_p, w_gate_b, w_up_b, w_down_b,
                      n_lat_tiles if last else n_tiles, n_lat_tiles)
    return xx.reshape(nb, SEQ, D_MODEL)
```

```python
import functools

import numpy as np
import jax
import jax.numpy as jnp
from jax import lax
from jax.experimental import pallas as pl
from jax.experimental.pallas import tpu as pltpu

D_MODEL = 1024
SEQ = 4096
CTX_LEN = 256
GRID_W = 64
ROWS = SEQ // GRID_W
HEAD_DIM = 64
A_HEADS = 6
B_HEADS = 4
C_HEADS = 6
GQA_G = 3
WINDOW = 128
NA_KH = 8
NA_KW = 16
FFN_HIDDEN = 2816
IN_WIDTH = 2048
ROPE_THETA = 10000.0
EPS = 1e-6
NEG_INF = -1e30
LOG2E = 1.4426950408889634
MOD_ROWS = 8
CTX_ROW = 4

LANE = 128
TM = 512
LAT_TILES = SEQ // TM
PROJ_GROUPS = 4
MXU_DIM = 256
FFN_SPLITS = (0, (FFN_HIDDEN // MXU_DIM + 1) // 2 * MXU_DIM, FFN_HIDDEN)
TQ = 128
NQ = SEQ // TQ
NQ_CTX = CTX_LEN // TQ
BPS = 2
Q_ROWS = TQ // GRID_W
NB_ROWS = Q_ROWS + NA_KH
NB_KEYS = NB_ROWS * GRID_W
A_SPAN = TQ + 2 * WINDOW
C_CHUNK = 1024
PIECE = 128
VMEM_LIMIT = 56 * 1024 * 1024

COL_QA, COL_QC, COL_KA, COL_KC = 0, 384, 768, 896
COL_QB, COL_KB = 1024, 1280
COL_VA, COL_VC, COL_VB = 1536, 1664, 1792
OCOL_A, OCOL_B, OCOL_C = 0, 384, 640

F32 = jnp.float32
BF16 = jnp.bfloat16
NT_DIMS = (((1,), (1,)), ((), ()))


def _gqa_cols(base):
    cols = np.empty(GQA_G * 2 * HEAD_DIM, np.int32)
    for j in range(GQA_G):
        for g in range(2):
            dst = j * LANE + g * HEAD_DIM
            cols[dst:dst + HEAD_DIM] = base + (g * GQA_G + j) * HEAD_DIM + np.arange(HEAD_DIM)
    return cols


def _in_perm():
    r = np.arange
    return np.concatenate([
        _gqa_cols(0),
        _gqa_cols(1408),
        384 + r(128),
        1792 + r(128),
        640 + r(256),
        896 + r(256),
        512 + r(128),
        1920 + r(128),
        1152 + r(256),
    ]).astype(np.int32)


def _out_perm():
    return np.concatenate([_gqa_cols(0), 384 + np.arange(256), _gqa_cols(640)]).astype(np.int32)


def _permute(w, perm, axis):
    cuts = [0] + [k for k in range(1, len(perm)) if perm[k] != perm[k - 1] + 1] + [len(perm)]
    parts = [lax.slice_in_dim(w, int(perm[a]), int(perm[b - 1]) + 1, axis=axis)
             for a, b in zip(cuts[:-1], cuts[1:])]
    return jnp.concatenate(parts, axis=axis)


def _gain_vector(qk):
    sc = HEAD_DIM ** -0.5 * LOG2E
    t = jnp.tile
    return jnp.concatenate([
        t(qk[0] * sc, 6), t(qk[4] * sc, 6), t(qk[1], 2), t(qk[5], 2),
        t(qk[2] * sc, 4), t(qk[3], 4), jnp.ones((512,), F32)])


def _rope_tables():
    t = jnp.arange(SEQ, dtype=jnp.int32)
    row = (t // GRID_W).astype(F32)
    col = (t % GRID_W).astype(F32)
    n_freq = HEAD_DIM // 4
    inv = ROPE_THETA ** (-jnp.arange(n_freq, dtype=F32) / n_freq)
    ang = jnp.concatenate([row[:, None] * inv[None, :], col[:, None] * inv[None, :]], axis=-1)
    cos, sin = jnp.cos(ang), jnp.sin(ang)
    cos = jnp.concatenate([cos, jnp.ones((TM, HEAD_DIM // 2), F32)], axis=0)
    sin = jnp.concatenate([sin, jnp.zeros((TM, HEAD_DIM // 2), F32)], axis=0)
    cos_t = jnp.tile(cos, (1, 2 * LANE // HEAD_DIM))
    sin_t = jnp.tile(jnp.concatenate([-sin, sin], axis=-1), (1, LANE // HEAD_DIM))
    return cos_t, sin_t


def _nb_row_classes():
    pats = []
    for i in range(NQ):
        ks = int(np.clip(Q_ROWS * i - NA_KH // 2, 0, ROWS - NB_ROWS))
        r = Q_ROWS * i + np.arange(Q_ROWS)[:, None]
        krow = ks + np.arange(NB_ROWS)[None, :]
        rs = np.clip(r - NA_KH // 2, 0, ROWS - NA_KH)
        valid = (krow >= rs) & (krow < rs + NA_KH)
        pats.append(np.where(valid, krow - r + (NA_KH - 1), -1))
    uniq, inv = np.unique(np.stack(pats).reshape(NQ, -1), axis=0, return_inverse=True)
    return uniq.reshape(-1, Q_ROWS, NB_ROWS), inv.reshape(-1).astype(np.int32)


def _bias_tables(rpb, row_classes):
    w = GRID_W
    pad = jnp.pad(rpb, ((0, 0), (0, 0), (0, 0), (w, w)), constant_values=NEG_INF)
    cols = jnp.stack([lax.slice_in_dim(pad, w + NA_KW - 1 - c, 2 * w + NA_KW - 1 - c, axis=3)
                      for c in range(w)], axis=3)
    c = np.arange(w)
    ws = np.clip(c - NA_KW // 2, 0, w - NA_KW)
    col_ok = (c[None, :] >= ws[:, None]) & (c[None, :] < ws[:, None] + NA_KW)
    cols = jnp.where(col_ok, cols, NEG_INF)
    neg = jnp.full(cols.shape[:2] + (w, w), NEG_INF, F32)
    tables = []
    for cls in row_classes:
        rows = [jnp.concatenate([cols[:, :, int(d)] if d >= 0 else neg for d in cls[a]], axis=-1)
                for a in range(Q_ROWS)]
        tables.append(jnp.concatenate(rows, axis=-2))
    return jnp.stack(tables, axis=1)


def _compiler_params(n_axes, flags=None):
    return pltpu.CompilerParams(dimension_semantics=("arbitrary",) * n_axes,
                                vmem_limit_bytes=VMEM_LIMIT, flags=flags)


def _mod_kernel(c_ref, w_ref, b_ref, o_ref):
    c = c_ref[...]
    a = (c / (1.0 + jnp.exp(-c))).astype(BF16)
    o_ref[0] = jnp.dot(a, w_ref[0].astype(BF16), preferred_element_type=F32) + b_ref[0]


def _modulation(c8, w_mod, b_mod):
    depth = w_mod.shape[0]
    tn = 2048
    return pl.pallas_call(
        _mod_kernel,
        out_shape=jax.ShapeDtypeStruct((depth, MOD_ROWS, 6 * D_MODEL), F32),
        grid=(depth, 6 * D_MODEL // tn),
        in_specs=[pl.BlockSpec((MOD_ROWS, D_MODEL), lambda l, n: (0, 0)),
                  pl.BlockSpec((1, D_MODEL, tn), lambda l, n: (l, 0, n)),
                  pl.BlockSpec((1, 1, tn), lambda l, n: (l, 0, n))],
        out_specs=pl.BlockSpec((1, MOD_ROWS, tn), lambda l, n: (l, 0, n)),
        compiler_params=_compiler_params(2),
        name="modulation",
    )(c8, w_mod, b_mod.reshape(depth, 1, 6 * D_MODEL))


def _mod_spec(layer, chunk, n_lat_tiles):
    return pl.BlockSpec(
        (None, 1, 1, D_MODEL),
        lambda i: (layer, jnp.where(i < n_lat_tiles, i // LAT_TILES, CTX_ROW), 0, chunk))


def _resident_spec(shape):
    return pl.BlockSpec(shape, lambda i: (0,) * len(shape), pipeline_mode=pl.Buffered(1))


def _layer_spec(layer, shape):
    return pl.BlockSpec((None,) + shape, lambda i: (layer,) + (0,) * len(shape),
                        pipeline_mode=pl.Buffered(1))


def _norm_modulate(x, g, scale, shift):
    ms = jnp.mean(x * x, axis=-1, keepdims=True)
    return x * lax.rsqrt(ms + EPS) * (g * (1.0 + scale)) + shift


def _inproj_kernel(x_ref, sh_ref, sc_ref, g_ref, w_ref, gv_ref, bd_ref, cos_ref, sin_ref, o_ref):
    _project_tile(x_ref[...], sh_ref, sc_ref, g_ref, w_ref, gv_ref, bd_ref, cos_ref, sin_ref, o_ref)


def _project_tile(x, sh_ref, sc_ref, g_ref, w_ref, gv_ref, bd_ref, cos_ref, sin_ref, o_ref):
    h = _norm_modulate(x, g_ref[...], sc_ref[0], sh_ref[0]).astype(BF16)
    lane = lax.broadcasted_iota(jnp.int32, (TM, LANE), 1)
    first_half = (lane % HEAD_DIM) < (HEAD_DIM // 2)
    cos = cos_ref[...]
    sin = sin_ref[...]
    cw = 2 * LANE
    gw = IN_WIDTH // PROJ_GROUPS
    for c in range(IN_WIDTH // cw):
        cs = slice(c * cw, (c + 1) * cw)
        if (c * cw) % gw == 0:
            proj = jnp.dot(h, w_ref[:, c * cw:c * cw + gw], preferred_element_type=F32)
        p = proj[:, (c * cw) % gw:(c * cw) % gw + cw]
        if c * cw < COL_VA:
            ms = jnp.dot((p * p).astype(BF16), bd_ref[...], preferred_element_type=F32)
            p = p * lax.rsqrt(ms + EPS) * gv_ref[:, cs]
        if c * cw < COL_QB:
            halves = []
            for s in range(2):
                xh = p[:, s * LANE:(s + 1) * LANE]
                partner = jnp.where(first_half,
                                    pltpu.roll(xh, LANE - HEAD_DIM // 2, 1),
                                    pltpu.roll(xh, HEAD_DIM // 2, 1))
                halves.append(xh * cos + partner * sin)
            p = jnp.concatenate(halves, axis=1)
        o_ref[:, cs] = p.astype(BF16)


def _in_projection(layer, xx, mod4, g_attn, w_in, gvec, bd, cos_t, sin_t, n_lat_tiles):
    ntok = xx.shape[0]
    rope_spec = pl.BlockSpec(
        (TM, LANE), lambda i: (jnp.where(i < n_lat_tiles, i % LAT_TILES, LAT_TILES), 0))
    return pl.pallas_call(
        _inproj_kernel,
        out_shape=jax.ShapeDtypeStruct((ntok, IN_WIDTH), BF16),
        grid=(ntok // TM,),
        in_specs=[pl.BlockSpec((TM, D_MODEL), lambda i: (i, 0)),
                  _mod_spec(layer, 0, n_lat_tiles), _mod_spec(layer, 1, n_lat_tiles),
                  _layer_spec(layer, (1, D_MODEL)), _layer_spec(layer, (D_MODEL, IN_WIDTH)),
                  _layer_spec(layer, (1, IN_WIDTH)), _resident_spec((2 * LANE, 2 * LANE)),
                  rope_spec, rope_spec],
        out_specs=pl.BlockSpec((TM, IN_WIDTH), lambda i: (i, 0)),
        compiler_params=_compiler_params(1),
        name="in_projection",
    )(xx, mod4, mod4, g_attn, w_in, gvec, bd, cos_t, sin_t)


def _stack_gqa(q_ref, nq):
    low = lax.broadcasted_iota(jnp.int32, (nq, LANE), 1) < HEAD_DIM
    blocks = []
    for g in range(2):
        for j in range(GQA_G):
            qj = q_ref[:, j * LANE:(j + 1) * LANE]
            blocks.append(jnp.where(low if g == 0 else jnp.logical_not(low), qj, jnp.zeros_like(qj)))
    return jnp.concatenate(blocks, axis=0)


def _unstack_gqa(o, nq):
    low = lax.broadcasted_iota(jnp.int32, (nq, LANE), 1) < HEAD_DIM
    return [jnp.where(low, o[j * nq:(j + 1) * nq], o[(GQA_G + j) * nq:(GQA_G + j + 1) * nq])
            for j in range(GQA_G)]


def _stack_mha(q_ref, nq):
    q = q_ref[...]
    head = lax.broadcasted_iota(jnp.int32, (nq, B_HEADS * HEAD_DIM), 1) // HEAD_DIM
    return jnp.concatenate([jnp.where(head == h, q, jnp.zeros_like(q)) for h in range(B_HEADS)], axis=0)


def _unstack_mha(o, nq):
    head = lax.broadcasted_iota(jnp.int32, (nq, B_HEADS * HEAD_DIM), 1) // HEAD_DIM
    out = jnp.where(head == 0, o[0:nq], 0.0)
    for h in range(1, B_HEADS):
        out = jnp.where(head == h, o[h * nq:(h + 1) * nq], out)
    return out


def _qk(q, k):
    return lax.dot_general(q, k, NT_DIMS, preferred_element_type=F32)


def _softmax_pv(parts, nq, sink_ref=None, adjust=None):
    rows = parts[0][0].shape[0]
    width = parts[0][1].shape[1]
    mxu_sums = width == LANE
    probs = [[] for _ in parts]
    dens = []
    for r0 in range(0, rows, PIECE):
        sl = [s[r0:r0 + PIECE] for s, _ in parts]
        if adjust is not None:
            sl[0] = adjust(sl[0], r0 // nq, r0 % nq)
        m = None
        for sp in sl:
            sm = jnp.max(sp, axis=-1, keepdims=True)
            m = sm if m is None else jnp.maximum(m, sm)
        den = None
        if sink_ref is not None:
            sink = sink_ref[r0 // nq] * LOG2E
            m = jnp.maximum(m, sink)
            den = jnp.exp2(sink - m)
        for k, sp in enumerate(sl):
            p = jnp.exp2(sp - m)
            if not mxu_sums:
                ps = jnp.sum(p, axis=-1, keepdims=True)
                den = ps if den is None else den + ps
            probs[k].append(p.astype(BF16))
        if den is not None:
            dens.append(den)
    acc = None
    for k, (_, v) in enumerate(parts):
        if mxu_sums:
            v = jnp.concatenate([v, jnp.ones(v.shape, BF16)], axis=1)
        pv = jnp.dot(jnp.concatenate(probs[k], axis=0), v, preferred_element_type=F32)
        acc = pv if acc is None else acc + pv
    if not mxu_sums:
        return acc * (1.0 / jnp.concatenate(dens, axis=0))
    den = acc[:, width:]
    if dens:
        den = den + jnp.concatenate(dens, axis=0)
    return acc[:, :width] * (1.0 / den)


def _global_attention(q, k_ref, v_ref, kx_ref, vx_ref):
    chunks = [(k_ref, v_ref, slice(k0, k0 + C_CHUNK)) for k0 in range(0, SEQ, C_CHUNK)]
    chunks.append((kx_ref, vx_ref, slice(0, CTX_LEN)))
    m = acc = None
    for kr, vr, ks in chunks:
        s = _qk(q, kr[ks, :])
        v_ext = jnp.concatenate([vr[ks, :], jnp.ones((ks.stop - ks.start, LANE), BF16)], axis=1)
        sm = jnp.max(s, axis=-1, keepdims=True)
        m_new = sm if m is None else jnp.maximum(m, sm)
        pv = jnp.dot(jnp.exp2(s - m_new).astype(BF16), v_ext, preferred_element_type=F32)
        acc = pv if m is None else acc * jnp.exp2(m - m_new) + pv
        m = m_new
    return acc[:, :LANE] * (1.0 / acc[:, LANE:])


def _attn_kernel(layer, tbl_ref, sink_all, qa_all, qc_all, qb_all,
                 ka_ref, va_ref, kc_ref, vc_ref, kb_ref, vb_ref,
                 kax_ref, vax_ref, kcx_ref, vcx_ref, kbx_ref, vbx_ref, *rest):
    del tbl_ref
    sink_ref = sink_all.at[layer]
    bias_refs, o_all = rest[:BPS], rest[BPS]
    step = pl.program_id(1)

    def store(o_ref, oa, ob, oc):
        for j, blk in enumerate(_unstack_gqa(oa, TQ)):
            o_ref[:, OCOL_A + j * LANE:OCOL_A + (j + 1) * LANE] = blk.astype(BF16)
        o_ref[:, OCOL_B:OCOL_C] = _unstack_mha(ob, TQ).astype(BF16)
        for j, blk in enumerate(_unstack_gqa(oc, TQ)):
            o_ref[:, OCOL_C + j * LANE:OCOL_C + (j + 1) * LANE] = blk.astype(BF16)

    def latent_block(i, qa_ref, qc_ref, qb_ref, bias_ref, o_ref):
        qa = _stack_gqa(qa_ref, TQ)
        start = pl.multiple_of(jnp.clip(i * TQ - WINDOW, 0, SEQ - A_SPAN), LANE)
        na = A_SPAN + CTX_LEN
        col = lax.broadcasted_iota(jnp.int32, (TQ, na), 1)
        qpos = i * TQ + lax.broadcasted_iota(jnp.int32, (TQ, na), 0)
        valid = (jnp.abs(qpos - (start + col)) <= WINDOW) | (col >= A_SPAN)
        k_all = jnp.concatenate([ka_ref[pl.ds(start, A_SPAN), :], kax_ref[...]], axis=0)
        v_all = jnp.concatenate([va_ref[pl.ds(start, A_SPAN), :], vax_ref[...]], axis=0)
        oa = _softmax_pv([(_qk(qa, k_all), v_all)], TQ, sink_ref=sink_ref,
                         adjust=lambda sp, n, q0: jnp.where(valid[q0:q0 + PIECE], sp, NEG_INF))

        qb = _stack_mha(qb_ref, TQ)
        krow0 = jnp.clip(Q_ROWS * i - NA_KH // 2, 0, ROWS - NB_ROWS)
        kstart = pl.multiple_of(krow0 * GRID_W, GRID_W)
        k_all = jnp.concatenate([kb_ref[pl.ds(kstart, NB_KEYS), :], kbx_ref[...]], axis=0)
        v_all = jnp.concatenate([vb_ref[pl.ds(kstart, NB_KEYS), :], vbx_ref[...]], axis=0)
        ob = _softmax_pv(
            [(_qk(qb, k_all), v_all)], TQ,
            adjust=lambda sp, n, q0: jnp.concatenate(
                [sp[:, :NB_KEYS] + bias_ref[0, n, q0:q0 + PIECE, :], sp[:, NB_KEYS:]], axis=1))

        oc = _global_attention(_stack_gqa(qc_ref, TQ), kc_ref, vc_ref, kcx_ref, vcx_ref)
        store(o_ref, oa, ob, oc)

    def context_block(qa_ref, qc_ref, qb_ref, o_ref):
        qa = _stack_gqa(qa_ref, TQ)
        oa = _softmax_pv([(_qk(qa, kax_ref[...]), vax_ref[...])], TQ, sink_ref=sink_ref)
        qb = _stack_mha(qb_ref, TQ)
        ob = _softmax_pv([(_qk(qb, kbx_ref[...]), vbx_ref[...])], TQ)
        qc = _stack_gqa(qc_ref, TQ)
        oc = _softmax_pv([(_qk(qc, kcx_ref[...]), vcx_ref[...])], TQ)
        store(o_ref, oa, ob, oc)

    blocks = [pl.ds(t * TQ, TQ) for t in range(BPS)]

    @pl.when(step < NQ // BPS)
    def _latent_queries():
        for t, rows in enumerate(blocks):
            latent_block(step * BPS + t, qa_all.at[rows], qc_all.at[rows], qb_all.at[rows],
                         bias_refs[t], o_all.at[rows])

    @pl.when(step >= NQ // BPS)
    def _context_queries():
        for rows in blocks:
            context_block(qa_all.at[rows], qc_all.at[rows], qb_all.at[rows], o_all.at[rows])


def _kv_cols():
    return [(128, COL_KA // 128), (128, COL_VA // 128), (128, COL_KC // 128), (128, COL_VC // 128),
            (256, COL_KB // 256), (256, COL_VB // 256)]


def _attention(layer, qkv, sink, bias_tbl, tbl_idx, nb, with_ctx):
    ctx_blk0 = nb * SEQ // CTX_LEN
    lat_steps, ctx_steps = NQ // BPS, NQ_CTX // BPS
    n_steps = lat_steps + (ctx_steps if with_ctx else 0)
    qrow = lambda b, i: jnp.where(i < lat_steps, b * lat_steps + i,
                                  nb * lat_steps + b * ctx_steps + i - lat_steps)
    qspec = lambda w, cb: pl.BlockSpec((BPS * TQ, w), lambda b, i, t: (qrow(b, i), cb))
    lat = lambda w, cb: pl.BlockSpec((SEQ, w), lambda b, i, t: (b, cb))
    cxt = lambda w, cb: pl.BlockSpec((CTX_LEN, w), lambda b, i, t: (ctx_blk0 + b, cb))
    bias = lambda k: pl.BlockSpec((None, 1, B_HEADS, TQ, NB_KEYS),
                                  lambda b, i, t: (layer, t[BPS * i + k], 0, 0, 0))
    grid_spec = pltpu.PrefetchScalarGridSpec(
        num_scalar_prefetch=1,
        grid=(nb, n_steps),
        in_specs=[pl.BlockSpec(memory_space=pltpu.SMEM),
                  qspec(384, COL_QA // 384), qspec(384, COL_QC // 384), qspec(256, COL_QB // 256)]
                 + [lat(w, cb) for w, cb in _kv_cols()] + [cxt(w, cb) for w, cb in _kv_cols()]
                 + [bias(k) for k in range(BPS)],
        out_specs=pl.BlockSpec((BPS * TQ, D_MODEL), lambda b, i, t: (qrow(b, i), 0)),
    )
    return pl.pallas_call(
        functools.partial(_attn_kernel, layer),
        out_shape=jax.ShapeDtypeStruct((nb * n_steps * BPS * TQ, D_MODEL), BF16),
        grid_spec=grid_spec,
        compiler_params=_compiler_params(2),
        name="attention",
    )(tbl_idx, sink, *([qkv] * 15), *([bias_tbl] * BPS))


def _ffn_kernel(o_ref, x_ref, gaa_ref, shf_ref, scf_ref, gaf_ref, g_ref,
                wo_ref, wg_ref, wu_ref, wd_ref, *rest):
    out_ref = rest[-1] if len(rest) == 1 else rest[-2]
    a = jnp.dot(o_ref[...], wo_ref[...], preferred_element_type=F32)
    x1 = x_ref[...] + gaa_ref[0] * a
    h = _norm_modulate(x1, g_ref[...], scf_ref[0], shf_ref[0]).astype(BF16)
    y = None
    for c0, c1 in zip(FFN_SPLITS[:-1], FFN_SPLITS[1:]):
        cs = slice(c0, c1)
        u = jnp.dot(h, wg_ref[:, cs], preferred_element_type=F32)
        v = jnp.dot(h, wu_ref[:, cs], preferred_element_type=F32)
        act = (u / (1.0 + jnp.exp(-u)) * v).astype(BF16)
        yc = jnp.dot(act, wd_ref[cs, :], preferred_element_type=F32)
        y = yc if y is None else y + yc
    x2 = x1 + gaf_ref[0] * y
    out_ref[...] = x2
    if len(rest) > 1:
        _project_tile(x2, *rest[:8], rest[-1])


def _out_ffn(layer, o, xx, mod4, g_ffn, w_out, w_gate, w_up, w_down, n_tiles, n_lat_tiles,
             next_proj=None):
    tile = pl.BlockSpec((TM, D_MODEL), lambda i: (i, 0))
    in_specs = ([tile, tile] + [_mod_spec(layer, k, n_lat_tiles) for k in (2, 3, 4, 5)]
                + [_layer_spec(layer, (1, D_MODEL)), _layer_spec(layer, (D_MODEL, D_MODEL)),
                   _layer_spec(layer, (D_MODEL, FFN_HIDDEN)),
                   _layer_spec(layer, (D_MODEL, FFN_HIDDEN)),
                   _layer_spec(layer, (FFN_HIDDEN, D_MODEL))])
    args = [o, xx, mod4, mod4, mod4, mod4, g_ffn, w_out, w_gate, w_up, w_down]
    out_shape = jax.ShapeDtypeStruct((n_tiles * TM, D_MODEL), F32)
    out_specs = tile
    if next_proj is not None:
        g_attn, w_in, gvec, bd, cos_t, sin_t = next_proj
        rope_spec = pl.BlockSpec(
            (TM, LANE), lambda i: (jnp.where(i < n_lat_tiles, i % LAT_TILES, LAT_TILES), 0))
        nxt = layer + 1
        in_specs += [_mod_spec(nxt, 0, n_lat_tiles), _mod_spec(nxt, 1, n_lat_tiles),
                     _layer_spec(nxt, (1, D_MODEL)), _layer_spec(nxt, (D_MODEL, IN_WIDTH)),
                     _layer_spec(nxt, (1, IN_WIDTH)), _resident_spec((2 * LANE, 2 * LANE)),
                     rope_spec, rope_spec]
        args += [mod4, mod4, g_attn, w_in, gvec, bd, cos_t, sin_t]
        out_shape = (out_shape, jax.ShapeDtypeStruct((n_tiles * TM, IN_WIDTH), BF16))
        out_specs = (tile, pl.BlockSpec((TM, IN_WIDTH), lambda i: (i, 0)))
    return pl.pallas_call(
        _ffn_kernel,
        out_shape=out_shape,
        grid=(n_tiles,),
        in_specs=in_specs,
        out_specs=out_specs,
        compiler_params=_compiler_params(1),
        name="out_ffn",
    )(*args)


def kernel(x, c, ctx, c_ctx, w_mod, b_mod, norm_attn, norm_ffn, w_in, qk_norm, sink, rpb,
           w_out, w_gate, w_up, w_down):
    depth = w_mod.shape[0]
    nb = x.shape[0]
    assert x.shape == (nb, SEQ, D_MODEL) and ctx.shape == (nb, CTX_LEN, D_MODEL) and nb <= CTX_ROW
    assert (nb * CTX_LEN) % TM == 0
    n_lat_tiles = nb * LAT_TILES
    n_tiles = n_lat_tiles + nb * CTX_LEN // TM

    c8 = jnp.zeros((MOD_ROWS, D_MODEL), F32).at[:nb].set(c).at[CTX_ROW].set(c_ctx)
    mod4 = _modulation(c8, w_mod, b_mod).reshape(depth, MOD_ROWS, 1, 6 * D_MODEL)
    g_attn, g_ffn = norm_attn[:, None, :], norm_ffn[:, None, :]
    gvec = jax.vmap(_gain_vector)(qk_norm)[:, None, :]

    w_in_p = _permute(w_in.astype(BF16), _in_perm(), 2)
    w_out_p = _permute(w_out.astype(BF16), _out_perm(), 1)
    w_gate_b, w_up_b, w_down_b = w_gate.astype(BF16), w_up.astype(BF16), w_down.astype(BF16)
    cos_t, sin_t = _rope_tables()
    head_id = np.arange(2 * LANE) // HEAD_DIM
    bd = jnp.asarray((head_id[:, None] == head_id[None, :]) / HEAD_DIM, BF16)
    row_classes, tbl_idx = _nb_row_classes()
    bias_all = _bias_tables(rpb * LOG2E, row_classes)
    tbl_idx = jnp.asarray(np.concatenate([tbl_idx, np.full(NQ_CTX, tbl_idx[-1], np.int32)]))

    xx = jnp.concatenate([x.reshape(nb * SEQ, D_MODEL), ctx.reshape(nb * CTX_LEN, D_MODEL)], axis=0)
    for l in range(depth):
        last = l == depth - 1
        if l == 0:
            qkv = _in_projection(l, xx, mod4, g_attn, w_in_p, gvec, bd, cos_t, sin_t, n_lat_tiles)
        o = _attention(l, qkv, sink, bias_all, tbl_idx, nb, not last)
        if last:
            xx = _out_ffn(l, o, xx, mod4, g_ffn, w_out_p, w_gate_b, w_up_b, w_down_b,
                          n_lat_tiles, n_lat_tiles)
        else:
            xx, qkv = _out_ffn(l, o, xx, mod4, g_ffn, w_out_p, w_gate_b, w_up_b, w_down_b,
                               n_tiles, n_lat_tiles,
                               next_proj=(g_attn, w_in_p, gvec, bd, cos_t, sin_t))
    return xx.reshape(nb, SEQ, D_MODEL)
```
